```python
import math
import jax, jax.numpy as jnp
from jax import lax
import numpy as np

D_MODEL = 1024
BATCH = 8
SEQ = 4096
DEPTH = 4

N_META = 16
GRID_W = 64
ROPE_THETA = 10000.0
EPS = 1e-6
EXP_CLIP = 30.0

ATT_HEADS = 8
ATT_KV_HEADS = 2
HEAD_DIM = 64
Q_BLOCK = 128
ATT_WIDTH = ATT_HEADS * HEAD_DIM
KV_WIDTH = ATT_KV_HEADS * HEAD_DIM

HGRN_HEADS = 4
HGRN_EXPAND = 128
HGRN_HEAD_V = 128
HGRN_KEY_WIDTH = HGRN_HEADS * HGRN_EXPAND
HGRN_WIDTH = HGRN_HEADS * HGRN_HEAD_V

SSM_HEADS = 8
SSM_HEAD_DIM = 64
SSM_GROUPS = 2
SSM_STATE = 64
SSM_CONV = 7
SSM_WIDTH = SSM_HEADS * SSM_HEAD_DIM
SSM_CONV_DIM = SSM_WIDTH + 2 * SSM_GROUPS * SSM_STATE

CHUNK = 64
MIX_WIDTH = ATT_WIDTH + HGRN_WIDTH + SSM_WIDTH

IN_SIZES = (ATT_WIDTH, KV_WIDTH, KV_WIDTH,
            HGRN_KEY_WIDTH, HGRN_KEY_WIDTH, HGRN_KEY_WIDTH,
            HGRN_WIDTH, HGRN_WIDTH,
            SSM_WIDTH, SSM_CONV_DIM, 2 * SSM_HEADS)
IN_WIDTH = sum(IN_SIZES)

N_EXPERTS = 16
EXPERT_FF = 2048
CAPACITY_FACTOR = 2

kernel_name = 'hybrid_parallel_heads_ec_moe_encoder'


def _rmsnorm(x, w):
    xf = x.astype(jnp.float32)
    y = xf * lax.rsqrt(jnp.mean(xf * xf, axis=-1, keepdims=True) + EPS)
    return (y * w.astype(jnp.float32)).astype(x.dtype)


def _axial_rope_tables(n_tok):
    rows = n_tok // GRID_W
    row = jnp.repeat(jnp.arange(rows), GRID_W).astype(jnp.float32)
    col = (jnp.arange(rows * GRID_W) % GRID_W).astype(jnp.float32)
    n_pair = HEAD_DIM // 4
    inv = ROPE_THETA ** (-jnp.arange(n_pair, dtype=jnp.float32) / n_pair)
    ang = jnp.concatenate([row[:, None] * inv, col[:, None] * inv], axis=-1)
    ang = jnp.concatenate([jnp.zeros((N_META, HEAD_DIM // 2), jnp.float32), ang], axis=0)
    return jnp.cos(ang), jnp.sin(ang)


def _apply_rope(x, cos, sin):
    half = HEAD_DIM // 2
    x1, x2 = x[..., :half], x[..., half:]
    c, s = cos[None, :, None, :], sin[None, :, None, :]
    return jnp.concatenate([x1 * c - x2 * s, x1 * s + x2 * c], axis=-1).astype(x.dtype)


def _attention_group(q, k, v, q_norm_w, k_norm_w, cos, sin):
    bsz, L, _ = q.shape
    rep = ATT_HEADS // ATT_KV_HEADS
    q = _apply_rope(_rmsnorm(q.reshape(bsz, L, ATT_HEADS, HEAD_DIM), q_norm_w), cos, sin)
    k = _apply_rope(_rmsnorm(k.reshape(bsz, L, ATT_KV_HEADS, HEAD_DIM), k_norm_w), cos, sin)
    q = q.reshape(bsz, L, ATT_KV_HEADS, rep, HEAD_DIM).transpose(0, 2, 3, 1, 4)
    k = k.transpose(0, 2, 1, 3)
    v = v.reshape(bsz, L, ATT_KV_HEADS, HEAD_DIM).transpose(0, 2, 1, 3)
    scale = HEAD_DIM ** -0.5

    def attend(qb):
        s = jnp.einsum('bgrqd,bgkd->bgrqk', qb, k).astype(jnp.float32) * scale
        p = jax.nn.softmax(s, axis=-1).astype(v.dtype)
        return jnp.einsum('bgrqk,bgkd->bgrqd', p, v)

    o_meta = attend(q[:, :, :, :N_META])
    n_real = L - N_META
    n_blk = n_real // Q_BLOCK
    qr = q[:, :, :, N_META:].reshape(bsz, ATT_KV_HEADS, rep, n_blk, Q_BLOCK, HEAD_DIM)
    o_real = lax.map(attend, jnp.moveaxis(qr, 3, 0))
    o_real = jnp.moveaxis(o_real, 0, 3).reshape(bsz, ATT_KV_HEADS, rep, n_real, HEAD_DIM)
    o = jnp.concatenate([o_meta, o_real], axis=3)
    return o.transpose(0, 3, 1, 2, 4).reshape(bsz, L, ATT_WIDTH)


def _to_chunks(a):
    bsz, h, t = a.shape[:3]
    return jnp.moveaxis(a.reshape((bsz, h, t // CHUNK, CHUNK) + a.shape[3:]), 2, 0)


def _from_chunks(o):
    o = jnp.moveaxis(o, 0, 2)
    return o.reshape(o.shape[0], o.shape[1], -1, o.shape[-1])


def _masked_exp(mask, diff):
    return jnp.where(mask, jnp.exp(jnp.where(mask, diff, 0.0)), 0.0)


def _gla_chunk_scan(q, k, v, g):
    bsz, h, _, dk = q.shape
    dv = v.shape[-1]
    mask = jnp.tril(jnp.ones((CHUNK, CHUNK), bool))

    def step(S, inp):
        qi, ki, vi, gi = inp
        cum = jnp.cumsum(gi, axis=2)
        rel = _masked_exp(mask[:, :, None], cum[:, :, :, None, :] - cum[:, :, None, :, :])
        att = jnp.einsum('bhid,bhjd,bhijd->bhij', qi, ki, rel)
        o = jnp.einsum('bhij,bhjv->bhiv', att, vi) + jnp.einsum('bhid,bhdv->bhiv', qi * jnp.exp(cum), S)
        last = cum[:, :, -1:, :]
        S = jnp.exp(last[:, :, 0, :])[..., None] * S + jnp.einsum('bhjd,bhjv->bhdv', ki * jnp.exp(last - cum), vi)
        return S, o

    S0 = jnp.zeros((bsz, h, dk, dv), jnp.float32)
    _, o = lax.scan(step, S0, (_to_chunks(q), _to_chunks(k), _to_chunks(v), _to_chunks(g)))
    return _from_chunks(o)


def _ssd_chunk_scan(q, k, v, g):
    bsz, h, _, n = q.shape
    p = v.shape[-1]
    mask = jnp.tril(jnp.ones((CHUNK, CHUNK), bool))

    def step(S, inp):
        qi, ki, vi, gi = inp
        cum = jnp.cumsum(gi, axis=-1)
        decay = _masked_exp(mask, cum[..., :, None] - cum[..., None, :])
        att = jnp.einsum('bhin,bhjn->bhij', qi, ki) * decay
        o = jnp.einsum('bhij,bhjp->bhip', att, vi) + jnp.einsum('bhin,bhnp->bhip', qi, S) * jnp.exp(cum)[..., None]
        S = jnp.exp(cum[..., -1])[..., None, None] * S + jnp.einsum(
            'bhjn,bhjp->bhnp', ki * jnp.exp(cum[..., -1:] - cum)[..., None], vi)
        return S, o

    gc = jnp.moveaxis(g.reshape(bsz, h, -1, CHUNK), 2, 0)
    S0 = jnp.zeros((bsz, h, n, p), jnp.float32)
    _, o = lax.scan(step, S0, (_to_chunks(q), _to_chunks(k), _to_chunks(v), gc))
    return _from_chunks(o)


def _bidirectional(scan_fn, q, v, k_f, g_f, k_b, g_b):
    pad = (-N_META) % CHUNK

    def prep(a):
        return jnp.pad(a.astype(jnp.float32), [(0, 0), (0, 0), (pad, 0)] + [(0, 0)] * (a.ndim - 3))

    q, v, k_f, g_f, k_b, g_b = (prep(a) for a in (q, v, k_f, g_f, k_b, g_b))
    rev = lambda a: jnp.flip(a, axis=2)
    o = scan_fn(q, k_f, v, g_f) + rev(scan_fn(rev(q), rev(k_b), rev(v), rev(g_b)))
    return o[:, :, pad:]


def _hgrn2_group(q, f_fwd, f_bwd, i, gate, lb_fwd, lb_bwd, norm_w):
    bsz, L, _ = q.shape
    heads = lambda a, d: a.reshape(bsz, L, HGRN_HEADS, d).transpose(0, 2, 1, 3)

    def log_forget(f_pre, lb):
        xf = f_pre.astype(jnp.float32)
        return jax.nn.log_sigmoid(xf) + jnp.log1p(lb * jnp.exp(jnp.minimum(-xf, EXP_CLIP)))

    g_f = heads(jnp.minimum(log_forget(f_fwd, lb_fwd), 0.0), HGRN_EXPAND)
    g_b = heads(jnp.minimum(log_forget(f_bwd, lb_bwd), 0.0), HGRN_EXPAND)
    k_f, k_b = -jnp.expm1(g_f), -jnp.expm1(g_b)
    qh = heads(jax.nn.silu(q), HGRN_EXPAND)
    vh = heads(i, HGRN_HEAD_V)
    o = _bidirectional(_gla_chunk_scan, qh, vh, k_f, g_f, k_b, g_b)
    o = o.transpose(0, 2, 1, 3).astype(q.dtype)
    o = _rmsnorm(o, norm_w.reshape(HGRN_HEADS, HGRN_HEAD_V)).reshape(bsz, L, HGRN_WIDTH)
    return o * jax.nn.silu(gate)


def _mamba2_group(z, xbc, dt_pre, conv_w, conv_b, dt_bias, a_log, d_skip, norm_w):
    bsz, L, _ = z.shape
    half = SSM_CONV // 2
    xbc = lax.conv_general_dilated(xbc, conv_w.T[:, None, :], window_strides=(1,),
                                   padding=[(half, half)], dimension_numbers=('NWC', 'WIO', 'NWC'),
                                   feature_group_count=SSM_CONV_DIM) + conv_b
    xbc = jax.nn.silu(xbc)
    xs, bm, cm = jnp.split(xbc, [SSM_WIDTH, SSM_WIDTH + SSM_GROUPS * SSM_STATE], axis=-1)
    xh = xs.reshape(bsz, L, SSM_HEADS, SSM_HEAD_DIM).transpose(0, 2, 1, 3)
    rep = SSM_HEADS // SSM_GROUPS
    expand = lambda a: jnp.repeat(a.reshape(bsz, L, SSM_GROUPS, SSM_STATE), rep, axis=2).transpose(0, 2, 1, 3)
    bh, ch = expand(bm), expand(cm)
    dt = jax.nn.softplus(dt_pre.astype(jnp.float32).reshape(bsz, L, 2, SSM_HEADS) + dt_bias)
    dt = dt.transpose(2, 0, 3, 1)
    A = -jnp.exp(a_log.astype(jnp.float32))
    g = dt * A[:, None, :, None]
    y = _bidirectional(_ssd_chunk_scan, ch, xh, bh * dt[0][..., None], g[0], bh * dt[1][..., None], g[1])
    y = y + d_skip[:, None, None] * xh
    y = y.transpose(0, 2, 1, 3).reshape(bsz, L, SSM_WIDTH).astype(z.dtype)
    return _rmsnorm(y * jax.nn.silu(z), norm_w)


def _expert_choice_ffn(h, router_w, w_gate, w_up, w_down):
    bsz, L, d = h.shape
    cap = CAPACITY_FACTOR * L // N_EXPERTS
    aff = jax.nn.softmax(jnp.einsum('bld,de->ble', h, router_w).astype(jnp.float32), axis=-1)
    gate, idx = lax.top_k(jnp.swapaxes(aff, 1, 2), cap)
    xs = jax.vmap(lambda hb, ib: hb[ib])(h, idx)
    a = jnp.einsum('becd,edf->becf', xs, w_gate)
    u = jnp.einsum('becd,edf->becf', xs, w_up)
    out = jnp.einsum('becf,efd->becd', jax.nn.silu(a) * u, w_down)
    out = out * gate[..., None].astype(out.dtype)
    return jax.vmap(lambda ob, ib: jax.ops.segment_sum(ob.reshape(-1, d), ib.reshape(-1), num_segments=L))(out, idx)


def setup_inputs(seed: int = 0) -> dict:
    key = jax.random.key(seed)
    ks = jax.random.split(key, 24)
    f32 = jnp.float32
    nrm = lambda k, shape, scale: scale * jax.random.normal(k, shape, f32)
    gain = lambda k, shape: 1.0 + 0.02 * jax.random.normal(k, shape, f32)
    dt0 = jnp.exp(jax.random.uniform(ks[10], (DEPTH, 2, SSM_HEADS), f32, math.log(1e-3), math.log(1e-1)))
    return {
        'x': jax.random.normal(ks[0], (BATCH, SEQ, D_MODEL), f32),
        'meta_tokens': nrm(ks[1], (N_META, D_MODEL), 1.0),
        'norm1_w': gain(ks[2], (DEPTH, D_MODEL)),
        'w_in': nrm(ks[3], (DEPTH, D_MODEL, IN_WIDTH), D_MODEL ** -0.5),
        'q_norm_w': gain(ks[4], (DEPTH, HEAD_DIM)),
        'k_norm_w': gain(ks[5], (DEPTH, HEAD_DIM)),
        'attn_norm_w': gain(ks[6], (DEPTH, ATT_WIDTH)),
        'hgrn_lb': nrm(ks[7], (2, DEPTH, HGRN_KEY_WIDTH), 0.1),
        'hgrn_norm_w': gain(ks[8], (DEPTH, HGRN_WIDTH)),
        'conv_w': nrm(ks[9], (DEPTH, SSM_CONV_DIM, SSM_CONV), SSM_CONV ** -0.5),
        'conv_b': nrm(ks[11], (DEPTH, SSM_CONV_DIM), 0.01),
        'dt_bias': dt0 + jnp.log(-jnp.expm1(-dt0)),
        'a_log': jnp.log(jax.random.uniform(ks[12], (DEPTH, 2, SSM_HEADS), f32, 1.0, 16.0)),
        'd_skip': gain(ks[13], (DEPTH, SSM_HEADS)),
        'ssm_norm_w': gain(ks[14], (DEPTH, SSM_WIDTH)),
        'w_out': nrm(ks[15], (DEPTH, MIX_WIDTH, D_MODEL), MIX_WIDTH ** -0.5),
        'norm2_w': gain(ks[16], (DEPTH, D_MODEL)),
        'router_w': nrm(ks[17], (DEPTH, D_MODEL, N_EXPERTS), D_MODEL ** -0.5),
        'w_gate': nrm(ks[18], (DEPTH, N_EXPERTS, D_MODEL, EXPERT_FF), D_MODEL ** -0.5),
        'w_up': nrm(ks[19], (DEPTH, N_EXPERTS, D_MODEL, EXPERT_FF), D_MODEL ** -0.5),
        'w_down': nrm(ks[20], (DEPTH, N_EXPERTS, EXPERT_FF, D_MODEL), EXPERT_FF ** -0.5),
    }


def reference(x, meta_tokens, norm1_w, w_in, q_norm_w, k_norm_w, attn_norm_w, hgrn_lb, hgrn_norm_w,
              conv_w, conv_b, dt_bias, a_log, d_skip, ssm_norm_w, w_out, norm2_w, router_w,
              w_gate, w_up, w_down):
    bsz, n_tok, _ = x.shape
    h = jnp.concatenate([jnp.broadcast_to(meta_tokens[None].astype(x.dtype), (bsz, N_META, D_MODEL)), x], axis=1)
    cos, sin = _axial_rope_tables(n_tok)
    p = jax.nn.softmax(hgrn_lb.astype(jnp.float32), axis=1)
    lower_bounds = jnp.clip(jnp.cumsum(p, axis=1) - p[:, :1], 0.0, 1.0)
    split_points = np.cumsum(IN_SIZES)[:-1].tolist()
    for l in range(DEPTH):
        u = _rmsnorm(h, norm1_w[l])
        proj = jnp.einsum('bld,dn->bln', u, w_in[l])
        aq, ak, av, hq, hf, hb, hi, hg, sz, sxbc, sdt = jnp.split(proj, split_points, axis=-1)
        att = _attention_group(aq, ak, av, q_norm_w[l], k_norm_w[l], cos, sin)
        rec = _hgrn2_group(hq, hf, hb, hi, hg, lower_bounds[0, l], lower_bounds[1, l], hgrn_norm_w[l])
        ssm = _mamba2_group(sz, sxbc, sdt, conv_w[l], conv_b[l], dt_bias[l], a_log[l], d_skip[l], ssm_norm_w[l])
        mix = jnp.concatenate([_rmsnorm(att, attn_norm_w[l]), rec, ssm], axis=-1)
        h = h + jnp.einsum('blm,md->bld', mix, w_out[l]).astype(h.dtype)
        h = h + _expert_choice_ffn(_rmsnorm(h, norm2_w[l]), router_w[l], w_gate[l], w_up[l], w_down[l]).astype(h.dtype)
    return h[:, N_META:]
```

```python
import functools

import jax
import jax.numpy as jnp
from jax import lax
from jax.experimental import pallas as pl
from jax.experimental.pallas import tpu as pltpu

F32 = jnp.float32
BF16 = jnp.bfloat16
I32 = jnp.int32

D_MODEL = 1024
N_META = 16
GRID_W = 64
ROPE_THETA = 10000.0
EPS = 1e-6
EXP_CLIP = 30.0

ATT_HEADS = 8
ATT_KV_HEADS = 2
HEAD_DIM = 64
ATT_WIDTH = ATT_HEADS * HEAD_DIM
KV_WIDTH = ATT_KV_HEADS * HEAD_DIM
ATT_REP = ATT_HEADS // ATT_KV_HEADS

HGRN_HEADS = 4
HGRN_DK = 128
HGRN_WIDTH = 512

SSM_HEADS = 8
SSM_HEAD_DIM = 64
SSM_GROUPS = 2
SSM_STATE = 64
SSM_CONV = 7
SSM_WIDTH = 512
SSM_CONV_DIM = 768
SSM_REP = SSM_HEADS // SSM_GROUPS

N_EXPERTS = 16
EXPERT_FF = 2048
CAPACITY_FACTOR = 2

LANE = 128
SUBLANE = 8
LEAD = 128
PADF = LEAD - N_META
NEG_BIG = -1e30
VMEM_LIMIT = 56 * 1024 * 1024
XE_WIDTH = D_MODEL + LANE


def _cparams(sem):
    return pltpu.CompilerParams(dimension_semantics=sem, vmem_limit_bytes=VMEM_LIMIT)


def _dot(a, b):
    return jnp.dot(a, b, preferred_element_type=F32)


def _dot_nt(a, b):
    return lax.dot_general(a, b, (((1,), (1,)), ((), ())), preferred_element_type=F32)


def _split_bf16(x, n):
    parts, r = [], x
    for _ in range(n):
        p = r.astype(BF16)
        parts.append(p)
        r = r - p.astype(F32)
    return parts


def _dot_const_lhs(c, x, n=3):
    out = None
    for p in _split_bf16(x, n):
        t = _dot(c, p)
        out = t if out is None else out + t
    return out


def _dot_const_rhs(x, c, n=3):
    out = None
    for p in _split_bf16(x, n):
        t = _dot(p, c)
        out = t if out is None else out + t
    return out


def _silu(x):
    return x * jax.nn.sigmoid(x)


def _row_tile(p):
    return 384 if p % 384 == 0 else LANE


def _inproj_body(has_y, *refs):
    if has_y:
        h_ref, y_ref, nw_ref = refs[:3]
        ws = refs[3:8]
        hn_ref = refs[8]
        outs = refs[9:14]
        h = h_ref[...] + y_ref[...]
        hn_ref[...] = h
    else:
        h_ref, nw_ref = refs[:2]
        ws = refs[2:7]
        outs = refs[7:12]
        h = h_ref[...]
    ms = jnp.mean(h * h, axis=-1, keepdims=True)
    u = (h * lax.rsqrt(ms + EPS) * nw_ref[...]).astype(BF16)
    for w_ref, o_ref in zip(ws, outs):
        o_ref[...] = _dot(u, w_ref[...])


def _inproj(h2, y2, nw, ws):
    m = h2.shape[0]
    tm = 256 if m % 256 == 0 else LANE
    has_y = y2 is not None
    row = pl.BlockSpec((tm, D_MODEL), lambda i: (i, 0))
    const = lambda a: pl.BlockSpec(a.shape, lambda i: (0, 0))
    in_specs = [row] + ([row] if has_y else []) + [const(nw)] + [const(w) for w in ws]
    out_shape = ([jax.ShapeDtypeStruct((m, D_MODEL), F32)] if has_y else []) + [
        jax.ShapeDtypeStruct((m, w.shape[1]), F32) for w in ws]
    out_specs = ([row] if has_y else []) + [pl.BlockSpec((tm, w.shape[1]), lambda i: (i, 0)) for w in ws]
    args = [h2] + ([y2] if has_y else []) + [nw] + list(ws)
    res = pl.pallas_call(
        functools.partial(_inproj_body, has_y),
        grid=(m // tm,), in_specs=in_specs, out_specs=out_specs, out_shape=out_shape,
        compiler_params=_cparams(("parallel",)), name="inproj")(*args)
    if has_y:
        return res[0], res[1:]
    return h2, res


def _attn_prep_body(x_ref, cq_ref, sq_ref, ck_ref, sk_ref, qw_ref, kw_ref, gq_ref, gk_ref,
                    q_ref, kt_ref, v_ref):
    x = x_ref[0]
    q = x[:, :ATT_WIDTH]
    k = x[:, ATT_WIDTH:ATT_WIDTH + KV_WIDTH]
    v = x[:, ATT_WIDTH + KV_WIDTH:]

    def head_norm(a, g_ref, w_ref):
        ms = _dot_const_rhs(a * a, g_ref[...], 2)
        return a * lax.rsqrt(ms + EPS) * w_ref[...]

    def rope(a, c_ref, s_ref):
        width = a.shape[1]
        lane = lax.broadcasted_iota(I32, a.shape, 1)
        first = (lane & (HEAD_DIM - 1)) < HEAD_DIM // 2
        partner = jnp.where(first, pltpu.roll(a, width - HEAD_DIM // 2, 1),
                            pltpu.roll(a, HEAD_DIM // 2, 1))
        return a * c_ref[...] + partner * s_ref[...]

    q_ref[0] = rope(head_norm(q, gq_ref, qw_ref), cq_ref, sq_ref).astype(BF16)
    kr = rope(head_norm(k, gk_ref, kw_ref), ck_ref, sk_ref)
    kt_ref[0] = kr.T.astype(BF16)
    for g in range(ATT_KV_HEADS):
        v_ref[0, g] = v[:, g * HEAD_DIM:(g + 1) * HEAD_DIM].astype(BF16)


def _attn_prep(att_p, tabs, qw, kw, gq, gk):
    b, p, _ = att_p.shape
    tp = _row_tile(p)
    cq, sq, ck, sk = tabs
    tab = lambda a: pl.BlockSpec((tp, a.shape[1]), lambda i, j: (j, 0))
    const = lambda a: pl.BlockSpec(a.shape, lambda i, j: (0, 0))
    return pl.pallas_call(
        _attn_prep_body, grid=(b, p // tp),
        in_specs=[pl.BlockSpec((1, tp, ATT_WIDTH + 2 * KV_WIDTH), lambda i, j: (i, j, 0)),
                  tab(cq), tab(sq), tab(ck), tab(sk), const(qw), const(kw), const(gq), const(gk)],
        out_specs=[pl.BlockSpec((1, tp, ATT_WIDTH), lambda i, j: (i, j, 0)),
                   pl.BlockSpec((1, KV_WIDTH, tp), lambda i, j: (i, 0, j)),
                   pl.BlockSpec((1, ATT_KV_HEADS, tp, HEAD_DIM), lambda i, j: (i, 0, j, 0))],
        out_shape=[jax.ShapeDtypeStruct((b, p, ATT_WIDTH), BF16),
                   jax.ShapeDtypeStruct((b, KV_WIDTH, p), BF16),
                   jax.ShapeDtypeStruct((b, ATT_KV_HEADS, p, HEAD_DIM), BF16)],
        compiler_params=_cparams(("parallel", "parallel")), name="attn_prep")(
            att_p, cq, sq, ck, sk, qw, kw, gq, gk)


def _attn_body(q_ref, kt_ref, v_ref, bias_ref, o_ref):
    kt = kt_ref[0]
    v = v_ref[0, 0]
    bias = bias_ref[...]
    outs = []
    for r in range(ATT_REP):
        qh = q_ref[0, :, r * HEAD_DIM:(r + 1) * HEAD_DIM]
        s = _dot(qh, kt) + bias
        m = jnp.max(s, axis=-1, keepdims=True)
        pexp = jnp.exp(s - m)
        l = jnp.sum(pexp, axis=-1, keepdims=True)
        outs.append(_dot(pexp.astype(BF16), v) / l)
    o_ref[0] = jnp.concatenate(outs, axis=-1)


def _attention(q, kt, v, bias):
    b, p, _ = q.shape
    tq = 256 if p % 256 == 0 else LANE
    gw = ATT_REP * HEAD_DIM
    return pl.pallas_call(
        _attn_body, grid=(b, ATT_KV_HEADS, p // tq),
        in_specs=[pl.BlockSpec((1, tq, gw), lambda i, g, j: (i, j, g)),
                  pl.BlockSpec((1, HEAD_DIM, p), lambda i, g, j: (i, g, 0)),
                  pl.BlockSpec((1, 1, p, HEAD_DIM), lambda i, g, j: (i, g, 0, 0)),
                  pl.BlockSpec((1, p), lambda i, g, j: (0, 0))],
        out_specs=pl.BlockSpec((1, tq, gw), lambda i, g, j: (i, j, g)),
        out_shape=jax.ShapeDtypeStruct((b, p, ATT_WIDTH), F32),
        compiler_params=_cparams(("parallel", "parallel", "parallel")), name="attention")(q, kt, v, bias)


def _gla_body(backward, finalize, *refs):
    if finalize:
        (q_ref, f_ref, v_ref, lb_ref, tri_ref, gate_ref, of_ref, nw_ref,
         o_ref, st_ref, cum_ref) = refs
    else:
        q_ref, f_ref, v_ref, lb_ref, tri_ref, o_ref, st_ref, cum_ref = refs
    c = pl.program_id(2)
    cc = pl.num_programs(2) - 1 - c if backward else c

    @pl.when(c == 0)
    def _():
        st_ref[...] = jnp.zeros_like(st_ref)

    n = LANE
    row = lax.broadcasted_iota(I32, (n, n), 0)
    col = lax.broadcasted_iota(I32, (n, n), 1)
    valid = (cc * n + row) >= PADF

    x = f_ref[0]
    lb = lb_ref[0]
    g = jnp.minimum(x, 0.0) - jnp.log1p(jnp.exp(-jnp.abs(x)))
    g = g + jnp.log1p(lb * jnp.exp(jnp.minimum(-x, EXP_CLIP)))
    g = jnp.where(valid, jnp.minimum(g, 0.0), 0.0)
    k = jnp.where(valid, 1.0 - jnp.exp(g), 0.0)
    q = _silu(q_ref[0])
    v = v_ref[0]

    cum = _dot_const_lhs(tri_ref[...], g, 3)
    cum_ref[...] = cum

    xor = row ^ col
    att = jnp.where(xor == 0, _dot_nt(q.astype(BF16), k.astype(BF16)), 0.0)
    g_up = pltpu.roll(g, n - 1, 0)
    g_dn = pltpu.roll(g, 1, 0)
    r4 = row & 3
    s = 1
    while s < n:
        qside = ((row & s) == 0) if backward else ((row & s) != 0)
        if s == 1:
            expo = jnp.where(qside, g, 0.0)
        elif s == 2:
            if backward:
                expo = jnp.where(r4 == 0, g + g_up, jnp.where(r4 == 1, g, jnp.where(r4 == 2, 0.0, g_dn)))
            else:
                expo = jnp.where(r4 == 0, g_up, jnp.where(r4 == 1, 0.0, jnp.where(r4 == 2, g, g + g_dn)))
        else:
            parts = []
            for blk in range(n // (2 * s)):
                mid = blk * 2 * s + (s if backward else s - 1)
                parts.append(jnp.broadcast_to(cum_ref[pl.ds(mid, 1), :], (2 * s, n)))
            ref_cum = parts[0] if len(parts) == 1 else jnp.concatenate(parts, axis=0)
            expo = jnp.where(qside, cum - ref_cum, ref_cum - cum)
        e = jnp.exp(expo)
        qt = jnp.where(qside, q * e, 0.0).astype(BF16)
        kt = jnp.where(qside, 0.0, k * e).astype(BF16)
        a = _dot_nt(qt, kt)
        att = att + (a if 2 * s == n else jnp.where(xor < 2 * s, a, 0.0))
        s *= 2

    edge = cum_ref[pl.ds(0 if backward else n - 1, 1), :]
    st = st_ref[...]
    o = _dot(att.astype(BF16), v.astype(BF16)) + _dot_nt((q * jnp.exp(cum)).astype(BF16), st.astype(BF16))
    khat = k * jnp.exp(edge - cum)
    st_ref[...] = st * jnp.exp(edge) + _dot(v.T.astype(BF16), khat.astype(BF16))

    if finalize:
        o = o + of_ref[0]
        ms = jnp.mean(o * o, axis=-1, keepdims=True)
        o = o * lax.rsqrt(ms + EPS) * nw_ref[0]
        o = o * _silu(gate_ref[0])
    o_ref[0] = o


def _gla(hg, lb, tri, backward, o_fwd=None, nw=None):
    b, p, _ = hg.shape
    nc = p // LANE
    finalize = o_fwd is not None
    cidx = (lambda c: nc - 1 - c) if backward else (lambda c: c)
    blk = lambda off: pl.BlockSpec((1, LANE, LANE), lambda i, h, c: (i, cidx(c), off + h))
    per_head = pl.BlockSpec((1, 1, LANE), lambda i, h, c: (h, 0, 0))
    in_specs = [blk(0), blk(HGRN_HEADS * (2 if backward else 1)), blk(3 * HGRN_HEADS), per_head,
                pl.BlockSpec((LANE, LANE), lambda i, h, c: (0, 0))]
    args = [hg, hg, hg, lb, tri]
    if finalize:
        in_specs += [blk(4 * HGRN_HEADS), blk(0), per_head]
        args += [hg, o_fwd, nw]
    return pl.pallas_call(
        functools.partial(_gla_body, backward, finalize), grid=(b, HGRN_HEADS, nc),
        in_specs=in_specs, out_specs=blk(0),
        out_shape=jax.ShapeDtypeStruct((b, p, HGRN_WIDTH), F32),
        scratch_shapes=[pltpu.VMEM((LANE, LANE), F32), pltpu.VMEM((LANE, LANE), F32)],
        compiler_params=_cparams(("parallel", "parallel", "arbitrary")),
        name="gla_bwd" if backward else "gla_fwd")(*args)


def _conv_body(cur_ref, prev_ref, next_ref, cw_ref, cb_ref, xs_ref, bc_ref, ext_ref):
    j = pl.program_id(1)
    tp = cur_ref.shape[1]
    half = SSM_CONV // 2
    ext_ref[0:SUBLANE, :] = jnp.where(j > 0, prev_ref[0], 0.0)
    ext_ref[SUBLANE:SUBLANE + tp, :] = cur_ref[0]
    ext_ref[SUBLANE + tp:2 * SUBLANE + tp, :] = jnp.where(j < pl.num_programs(1) - 1, next_ref[0], 0.0)
    acc = jnp.broadcast_to(cb_ref[...], (tp, SSM_CONV_DIM))
    for w in range(SSM_CONV):
        acc = acc + ext_ref[pl.ds(SUBLANE - half + w, tp), :] * cw_ref[w:w + 1, :]
    y = _silu(acc)
    xs_ref[0] = y[:, :SSM_WIDTH]
    bc_ref[0] = y[:, SSM_WIDTH:]


def _ssd_conv(xbc, cw, cb):
    b, p, _ = xbc.shape
    tp = _row_tile(p)
    r8 = tp // SUBLANE
    last8 = p // SUBLANE - 1
    return pl.pallas_call(
        _conv_body, grid=(b, p // tp),
        in_specs=[pl.BlockSpec((1, tp, SSM_CONV_DIM), lambda i, j: (i, j, 0)),
                  pl.BlockSpec((1, SUBLANE, SSM_CONV_DIM), lambda i, j: (i, jnp.maximum(j * r8 - 1, 0), 0)),
                  pl.BlockSpec((1, SUBLANE, SSM_CONV_DIM), lambda i, j: (i, jnp.minimum((j + 1) * r8, last8), 0)),
                  pl.BlockSpec(cw.shape, lambda i, j: (0, 0)),
                  pl.BlockSpec(cb.shape, lambda i, j: (0, 0))],
        out_specs=[pl.BlockSpec((1, tp, SSM_WIDTH), lambda i, j: (i, j, 0)),
                   pl.BlockSpec((1, tp, 2 * SSM_GROUPS * SSM_STATE), lambda i, j: (i, j, 0))],
        out_shape=[jax.ShapeDtypeStruct((b, p, SSM_WIDTH), F32),
                   jax.ShapeDtypeStruct((b, p, 2 * SSM_GROUPS * SSM_STATE), F32)],
        scratch_shapes=[pltpu.VMEM((tp + 2 * SUBLANE, SSM_CONV_DIM), F32)],
        compiler_params=_cparams(("parallel", "parallel")), name="ssd_conv")(xbc, xbc, xbc, cw, cb)


def _ssd_body(backward, xs_ref, bc_ref, dt_ref, dtb_ref, a_ref, tri_ref, trit_ref, y_ref, st_ref):
    c = pl.program_id(1)
    cc = pl.num_programs(1) - 1 - c if backward else c

    @pl.when(c == 0)
    def _():
        st_ref[...] = jnp.zeros_like(st_ref)

    n = LANE
    row = lax.broadcasted_iota(I32, (n, n), 0)
    col = lax.broadcasted_iota(I32, (n, n), 1)
    valid = (cc * n + row) >= PADF
    tri = (col >= row) if backward else (col <= row)

    dt = jnp.where(valid, jax.nn.softplus(dt_ref[0] + dtb_ref[...]), 0.0)
    g = dt * a_ref[...]
    cum_col = _dot_const_lhs(tri_ref[...], g, 3)
    g_t = g.T
    dt_t = dt.T
    cum_row = _dot_const_rhs(g_t, trit_ref[...], 3)
    edge_t = 0 if backward else n - 1

    bc = bc_ref[0]
    b_t = bc[:, :SSM_GROUPS * SSM_STATE].T
    cm = bc[:, SSM_GROUPS * SSM_STATE:]
    xs = xs_ref[0]
    outs = []
    for grp in range(SSM_GROUPS):
        c_g = cm[:, grp * SSM_STATE:(grp + 1) * SSM_STATE].astype(BF16)
        bt_g = b_t[grp * SSM_STATE:(grp + 1) * SSM_STATE, :]
        cb = _dot(c_g, bt_g.astype(BF16))
        for hh in range(SSM_REP):
            head = grp * SSM_REP + hh
            ln = (SSM_HEADS if backward else 0) + head
            colv = cum_col[:, ln:ln + 1]
            rowv = cum_row[ln:ln + 1, :]
            dtrow = dt_t[ln:ln + 1, :]
            diff = colv - rowv
            decay = jnp.where(tri, jnp.exp(jnp.where(tri, diff, 0.0)), 0.0)
            att = cb * (decay * dtrow)
            x_h = xs[:, head * SSM_HEAD_DIM:(head + 1) * SSM_HEAD_DIM].astype(BF16)
            st = st_ref[head]
            o = _dot(att.astype(BF16), x_h) + _dot(c_g, st.astype(BF16)) * jnp.exp(colv)
            edge = rowv[:, edge_t:edge_t + 1]
            wrow = dtrow * jnp.exp(edge - rowv)
            st_ref[head] = jnp.exp(edge) * st + _dot((bt_g * wrow).astype(BF16), x_h)
            outs.append(o)
    y_ref[0] = jnp.concatenate(outs, axis=-1)


def _ssd(xs, bc, dtp, dtb, arow, tri, trit, backward):
    b, p, _ = xs.shape
    nc = p // LANE
    cidx = (lambda c: nc - 1 - c) if backward else (lambda c: c)
    blk = lambda w: pl.BlockSpec((1, LANE, w), lambda i, c: (i, cidx(c), 0))
    const = lambda a: pl.BlockSpec(a.shape, lambda i, c: (0, 0))
    return pl.pallas_call(
        functools.partial(_ssd_body, backward), grid=(b, nc),
        in_specs=[blk(SSM_WIDTH), blk(2 * SSM_GROUPS * SSM_STATE), blk(LANE),
                  const(dtb), const(arow), const(tri), const(trit)],
        out_specs=blk(SSM_WIDTH),
        out_shape=jax.ShapeDtypeStruct((b, p, SSM_WIDTH), F32),
        scratch_shapes=[pltpu.VMEM((SSM_HEADS, SSM_STATE, SSM_HEAD_DIM), F32)],
        compiler_params=_cparams(("parallel", "arbitrary")),
        name="ssd_bwd" if backward else "ssd_fwd")(xs, bc, dtp, dtb, arow, tri, trit)


def _outproj_body(att_ref, rec_ref, yf_ref, yb_ref, xs_ref, z_ref, h_ref, anw_ref, snw_ref, dsk_ref,
                  wa_ref, wr_ref, ws_ref, o_ref):
    tm = h_ref.shape[1]
    j = pl.program_id(1)

    def rms(a, w_ref):
        ms = jnp.mean(a * a, axis=-1, keepdims=True)
        return a * lax.rsqrt(ms + EPS) * w_ref[...]

    att = rms(att_ref[0], anw_ref)
    y = yf_ref[0] + yb_ref[0] + dsk_ref[...] * xs_ref[0]
    ssm = rms(y * _silu(z_ref[0]), snw_ref)
    out = (_dot(att.astype(BF16), wa_ref[...]) + _dot(rec_ref[0].astype(BF16), wr_ref[...])
           + _dot(ssm.astype(BF16), ws_ref[...]))
    rowi = j * tm + lax.broadcasted_iota(I32, (tm, 1), 0)
    o_ref[0] = jnp.where(rowi >= PADF, h_ref[0] + out, 0.0)


def _outproj(att, rec, yf, yb, xs, z, h, anw, snw, dsk, wa, wr, ws):
    b, p, _ = h.shape
    tm = _row_tile(p)
    blk = lambda w: pl.BlockSpec((1, tm, w), lambda i, j: (i, j, 0))
    const = lambda a: pl.BlockSpec(a.shape, lambda i, j: (0, 0))
    return pl.pallas_call(
        _outproj_body, grid=(b, p // tm),
        in_specs=[blk(512)] * 6 + [blk(D_MODEL), const(anw), const(snw), const(dsk),
                                   const(wa), const(wr), const(ws)],
        out_specs=blk(D_MODEL), out_shape=jax.ShapeDtypeStruct((b, p, D_MODEL), F32),
        compiler_params=_cparams(("parallel", "parallel")), name="outproj")(
            att, rec, yf, yb, xs, z, h, anw, snw, dsk, wa, wr, ws)


def _router_body(h_ref, nw_ref, wh_ref, wl_ref, xe_ref, afft_ref):
    tp = h_ref.shape[1]
    j = pl.program_id(1)
    h = h_ref[0]
    ms = jnp.mean(h * h, axis=-1, keepdims=True)
    xn = h * lax.rsqrt(ms + EPS) * nw_ref[...]
    x_hi = xn.astype(BF16)
    x_lo = (xn - x_hi.astype(F32)).astype(BF16)
    logits = _dot(x_hi, wh_ref[...]) + _dot(x_hi, wl_ref[...]) + _dot(x_lo, wh_ref[...])
    lane = lax.broadcasted_iota(I32, (tp, LANE), 1)
    real = lane < N_EXPERTS
    m = jnp.max(jnp.where(real, logits, NEG_BIG), axis=-1, keepdims=True)
    pexp = jnp.exp(jnp.where(real, logits - m, 0.0))
    aff = pexp / jnp.sum(jnp.where(real, pexp, 0.0), axis=-1, keepdims=True)
    aff = jnp.where(real, aff, 0.0)
    xe_ref[0, :, :D_MODEL] = x_hi.astype(F32)
    xe_ref[0, :, D_MODEL:] = aff
    rowi = j * tp + lax.broadcasted_iota(I32, (tp, LANE), 0)
    afft_ref[0] = jnp.where(rowi >= PADF, aff, -1.0).T[:N_EXPERTS, :]


def _router(h, nw, wh, wl):
    b, p, _ = h.shape
    tp = _row_tile(p)
    const = lambda a: pl.BlockSpec(a.shape, lambda i, j: (0, 0))
    return pl.pallas_call(
        _router_body, grid=(b, p // tp),
        in_specs=[pl.BlockSpec((1, tp, D_MODEL), lambda i, j: (i, j, 0)), const(nw), const(wh), const(wl)],
        out_specs=[pl.BlockSpec((1, tp, XE_WIDTH), lambda i, j: (i, j, 0)),
                   pl.BlockSpec((1, N_EXPERTS, tp), lambda i, j: (i, 0, j))],
        out_shape=[jax.ShapeDtypeStruct((b, p, XE_WIDTH), F32),
                   jax.ShapeDtypeStruct((b, N_EXPERTS, p), F32)],
        compiler_params=_cparams(("parallel", "parallel")), name="router")(h, nw, wh, wl)


def _select_body(cap, a2_ref, a3_ref, us_ref, ls_ref, li_ref, idx_ref, thr_ref, need_ref):
    n = LANE
    cps = idx_ref.shape[2]
    a2 = a2_ref[0]
    bits = lax.bitcast_convert_type(a2, I32)

    def search(i, t):
        cand = t | jnp.left_shift(jnp.int32(1), 30 - i)
        cnt = jnp.sum(jnp.where(bits >= cand, 1.0, 0.0), axis=1, keepdims=True)
        return jnp.where(cnt >= cap, cand, t)

    thr = lax.fori_loop(0, 31, search, jnp.zeros((N_EXPERTS, 1), I32))
    n_gt = jnp.sum(jnp.where(bits > thr, 1.0, 0.0), axis=1, keepdims=True)
    thr_ref[...] = jnp.broadcast_to(lax.bitcast_convert_type(thr, F32), (N_EXPERTS, n))
    need_ref[...] = jnp.broadcast_to(cap - n_gt, (N_EXPERTS, n))

    ones = jnp.ones((n, n), BF16)
    us = us_ref[...]
    ls = ls_ref[...]
    li = li_ref[...]
    c_row = lax.broadcasted_iota(I32, (1, cps), 1).astype(F32)
    c_full = lax.broadcasted_iota(I32, (n, cps), 1).astype(F32)
    k_full = lax.broadcasted_iota(I32, (n, cps), 0).astype(F32)
    reps = cps // n

    def per_expert(e, carry):
        a = a3_ref[0, e]
        vt = jnp.broadcast_to(thr_ref[pl.ds(e, 1), :], (n, n))
        nd = jnp.broadcast_to(need_ref[pl.ds(e, 1), :], (n, n))
        gt = jnp.where(a > vt, 1.0, 0.0)
        eq = jnp.where(a == vt, 1.0, 0.0).astype(BF16)
        eq_before = _dot(eq, us) + _dot(ls, _dot(eq, ones).astype(BF16))
        sel = gt + jnp.where(eq_before < nd, eq.astype(F32), 0.0)
        selb = sel.astype(BF16)
        tile_cnt = _dot(selb, ones)
        off_lo = _dot(ls, tile_cnt.astype(BF16))
        off_hi = off_lo + tile_cnt
        lo = jnp.concatenate([off_lo] * reps, axis=1)
        hi = jnp.concatenate([off_hi] * reps, axis=1)
        in_tile = jnp.where(lo <= c_full, jnp.where(c_full < hi, 1.0, 0.0), 0.0)
        incl = _dot(li, sel.T.astype(BF16))
        cnt_at = _dot(incl.astype(BF16), in_tile.astype(BF16))
        tile_of = jnp.sum(in_tile * k_full, axis=0, keepdims=True)
        rank = c_row - jnp.sum(in_tile * lo, axis=0, keepdims=True)
        local = jnp.sum(jnp.where(cnt_at <= rank, 1.0, 0.0), axis=0, keepdims=True)
        idx = jnp.where(c_row < cap, tile_of * n + local, 0.0)
        idx_ref[0, pl.ds(e, 1), :] = idx.astype(I32)
        return carry

    lax.fori_loop(0, N_EXPERTS, per_expert, 0)


def _select(afft, a3, us, ls, li, cap, cps):
    b, _, p = afft.shape
    const = lambda a: pl.BlockSpec(a.shape, lambda i: (0, 0))
    return pl.pallas_call(
        functools.partial(_select_body, cap), grid=(b,),
        in_specs=[pl.BlockSpec((1, N_EXPERTS, p), lambda i: (i, 0, 0)),
                  pl.BlockSpec((1, N_EXPERTS, LANE, LANE), lambda i: (i, 0, 0, 0)),
                  const(us), const(ls), const(li)],
        out_specs=pl.BlockSpec((1, N_EXPERTS, cps), lambda i: (i, 0, 0)),
        out_shape=jax.ShapeDtypeStruct((b, N_EXPERTS, cps), I32),
        scratch_shapes=[pltpu.VMEM((N_EXPERTS, LANE), F32), pltpu.VMEM((N_EXPERTS, LANE), F32)],
        compiler_params=_cparams(("parallel",)), name="select")(afft, a3, us, ls, li)


def _ffn_body(cap, idx_ref, xe_ref, wg_ref, wu_ref, wd_ref, y_ref, xs_ref, acc_ref, sem):
    b = pl.program_id(0)
    e = pl.program_id(1)
    f = pl.program_id(2)
    nf = pl.num_programs(2)
    rows = xs_ref.shape[0]

    def row_copy(c, src_row):
        return pltpu.make_async_copy(xe_ref.at[b, pl.ds(src_row, 1), :], xs_ref.at[pl.ds(c, 1), :], sem)

    @pl.when((e == 0) & (f == 0))
    def _():
        y_ref[...] = jnp.zeros_like(y_ref)

    @pl.when(f == 0)
    def _():
        def issue(c, carry):
            row_copy(c, idx_ref[0, 0, 0, c]).start()
            return carry

        def drain(c, carry):
            row_copy(c, 0).wait()
            return carry

        lax.fori_loop(0, rows, issue, 0)
        lax.fori_loop(0, rows, drain, 0)

    x = xs_ref[:, :D_MODEL].astype(BF16)
    a = _dot(x, wg_ref[0])
    u = _dot(x, wu_ref[0])
    part = _dot((_silu(a) * u).astype(BF16), wd_ref[0])

    @pl.when(f == 0)
    def _():
        acc_ref[...] = part

    @pl.when(f > 0)
    def _():
        acc_ref[...] += part

    @pl.when(f == nf - 1)
    def _():
        lane = lax.broadcasted_iota(I32, (rows, LANE), 1)
        gate = jnp.sum(jnp.where(lane == e, xs_ref[:, D_MODEL:], 0.0), axis=-1, keepdims=True)
        rowi = lax.broadcasted_iota(I32, (rows, 1), 0)
        acc_ref[...] = acc_ref[...] * jnp.where(rowi < cap, gate, 0.0)

        def scatter(c, carry):
            t = idx_ref[0, 0, 0, c]
            y_ref[0, pl.ds(t, 1), :] = y_ref[0, pl.ds(t, 1), :] + acc_ref[pl.ds(c, 1), :]
            return carry

        lax.fori_loop(0, rows, scatter, 0)


def _ffn(idx4, xe, wg, wu, wd, cap, rows):
    b, p, _ = xe.shape
    tf = 512
    nf = EXPERT_FF // tf
    return pl.pallas_call(
        functools.partial(_ffn_body, cap), grid=(b, N_EXPERTS, nf),
        in_specs=[pl.BlockSpec((1, 1, 1, idx4.shape[3]), lambda i, e, f: (i, e, 0, 0),
                               memory_space=pltpu.SMEM),
                  pl.BlockSpec(memory_space=pl.ANY),
                  pl.BlockSpec((1, D_MODEL, tf), lambda i, e, f: (e, 0, f)),
                  pl.BlockSpec((1, D_MODEL, tf), lambda i, e, f: (e, 0, f)),
                  pl.BlockSpec((1, tf, D_MODEL), lambda i, e, f: (e, f, 0))],
        out_specs=pl.BlockSpec((1, p, D_MODEL), lambda i, e, f: (i, 0, 0)),
        out_shape=jax.ShapeDtypeStruct((b, p, D_MODEL), F32),
        scratch_shapes=[pltpu.VMEM((rows, XE_WIDTH), F32), pltpu.VMEM((rows, D_MODEL), F32),
                        pltpu.SemaphoreType.DMA(())],
        compiler_params=_cparams(("parallel", "arbitrary", "arbitrary")), name="expert_ffn")(
            idx4, xe, wg, wu, wd)


def _final_body(h_ref, y_ref, o_ref):
    o_ref[...] = h_ref[...] + y_ref[...]


def _final_add(h, y, seq):
    b, p, _ = h.shape
    lead_blocks = LEAD // LANE
    src = pl.BlockSpec((1, LANE, D_MODEL), lambda i, j: (i, j + lead_blocks, 0))
    return pl.pallas_call(
        _final_body, grid=(b, seq // LANE), in_specs=[src, src],
        out_specs=pl.BlockSpec((1, LANE, D_MODEL), lambda i, j: (i, j, 0)),
        out_shape=jax.ShapeDtypeStruct((b, seq, D_MODEL), F32),
        compiler_params=_cparams(("parallel", "parallel")), name="final_add")(h, y)


def _rope_tables(seq):
    rows = seq // GRID_W
    row = jnp.repeat(jnp.arange(rows), GRID_W).astype(F32)
    col = (jnp.arange(rows * GRID_W) % GRID_W).astype(F32)
    n_pair = HEAD_DIM // 4
    inv = ROPE_THETA ** (-jnp.arange(n_pair, dtype=F32) / n_pair)
    ang = jnp.concatenate([row[:, None] * inv, col[:, None] * inv], axis=-1)
    ang = jnp.concatenate([jnp.zeros((LEAD, HEAD_DIM // 2), F32), ang], axis=0)
    cos, sin = jnp.cos(ang), jnp.sin(ang)
    cos64 = jnp.concatenate([cos, cos], axis=-1)
    sin64 = jnp.concatenate([-sin, sin], axis=-1)
    scale = HEAD_DIM ** -0.5
    return (jnp.tile(cos64, (1, ATT_HEADS)) * scale, jnp.tile(sin64, (1, ATT_HEADS)) * scale,
            jnp.tile(cos64, (1, ATT_KV_HEADS)), jnp.tile(sin64, (1, ATT_KV_HEADS)))


def _block_mean_matrix(width):
    i = jnp.arange(width)
    return jnp.where((i[:, None] // HEAD_DIM) == (i[None, :] // HEAD_DIM), 1.0 / HEAD_DIM, 0.0).astype(BF16)


def kernel(x, meta_tokens, norm1_w, w_in, q_norm_w, k_norm_w, attn_norm_w, hgrn_lb, hgrn_norm_w, conv_w, conv_b, dt_bias, a_log, d_skip, ssm_norm_w, w_out, norm2_w, router_w, w_gate, w_up, w_down):
    bsz, seq, _ = x.shape
    depth = norm1_w.shape[0]
    assert seq % LANE == 0 and seq % GRID_W == 0
    p = LEAD + seq
    assert p // LANE <= LANE
    n_tok = N_META + seq
    cap = CAPACITY_FACTOR * n_tok // N_EXPERTS
    rows_ffn = -(-cap // 16) * 16
    cps = -(-cap // LANE) * LANE

    h = jnp.concatenate([jnp.zeros((bsz, PADF, D_MODEL), x.dtype),
                         jnp.broadcast_to(meta_tokens[None].astype(x.dtype), (bsz, N_META, D_MODEL)), x], axis=1)

    tabs = _rope_tables(seq)
    tok = jnp.arange(p)
    att_bias = jnp.where(tok >= PADF, 0.0, NEG_BIG).astype(F32)[None, :]
    gq = _block_mean_matrix(ATT_WIDTH)
    gk = _block_mean_matrix(KV_WIDTH)
    ii = jnp.arange(LANE)
    tri_lo = (ii[None, :] <= ii[:, None]).astype(BF16)
    tri_up = (ii[None, :] >= ii[:, None]).astype(BF16)
    strict_up = (ii[:, None] < ii[None, :]).astype(BF16)
    strict_lo = (ii[None, :] < ii[:, None]).astype(BF16)

    soft = jax.nn.softmax(hgrn_lb.astype(F32), axis=1)
    lower_bounds = jnp.clip(jnp.cumsum(soft, axis=1) - soft[:, :1], 0.0, 1.0)

    y = None
    for l in range(depth):
        w = w_in[l]
        o = 0
        pieces = []
        for width in (ATT_WIDTH + 2 * KV_WIDTH, 5 * HGRN_WIDTH, SSM_WIDTH, SSM_CONV_DIM, 2 * SSM_HEADS):
            pieces.append(w[:, o:o + width])
            o += width
        pieces[4] = jnp.pad(pieces[4], ((0, 0), (0, LANE - 2 * SSM_HEADS)))
        ws = [a.astype(BF16) for a in pieces]
        h2, (att_p, hg_p, z_p, xbc_p, dt_p) = _inproj(
            h.reshape(bsz * p, D_MODEL), None if y is None else y.reshape(bsz * p, D_MODEL),
            norm1_w[l][None, :], ws)
        h = h2.reshape(bsz, p, D_MODEL)
        r3 = lambda a: a.reshape(bsz, p, a.shape[-1])
        att_p, hg_p, z_p, xbc_p, dt_p = r3(att_p), r3(hg_p), r3(z_p), r3(xbc_p), r3(dt_p)

        qr, kt, vv = _attn_prep(att_p, tabs, jnp.tile(q_norm_w[l], ATT_HEADS)[None, :],
                                jnp.tile(k_norm_w[l], ATT_KV_HEADS)[None, :], gq, gk)
        att = _attention(qr, kt, vv, att_bias)

        lb_f = lower_bounds[0, l].reshape(HGRN_HEADS, 1, HGRN_DK)
        lb_b = lower_bounds[1, l].reshape(HGRN_HEADS, 1, HGRN_DK)
        o_f = _gla(hg_p, lb_f, tri_lo, False)
        rec = _gla(hg_p, lb_b, tri_up, True, o_f, hgrn_norm_w[l].reshape(HGRN_HEADS, 1, HGRN_DK))

        cw = jnp.pad(conv_w[l].T, ((0, SUBLANE - SSM_CONV), (0, 0)))
        xs, bc = _ssd_conv(xbc_p, cw, conv_b[l][None, :])
        dtb = jnp.pad(dt_bias[l].reshape(-1), (0, LANE - 2 * SSM_HEADS))[None, :]
        arow = jnp.pad(-jnp.exp(a_log[l].astype(F32)).reshape(-1), (0, LANE - 2 * SSM_HEADS))[None, :]
        y_f = _ssd(xs, bc, dt_p, dtb, arow, tri_lo, tri_up, False)
        y_b = _ssd(xs, bc, dt_p, dtb, arow, tri_up, tri_lo, True)

        wo = w_out[l].astype(BF16)
        h = _outproj(att, rec, y_f, y_b, xs, z_p, h, attn_norm_w[l][None, :], ssm_norm_w[l][None, :],
                     jnp.repeat(d_skip[l], SSM_HEAD_DIM)[None, :],
                     wo[:ATT_WIDTH], wo[ATT_WIDTH:ATT_WIDTH + HGRN_WIDTH], wo[ATT_WIDTH + HGRN_WIDTH:])

        wr = jnp.pad(router_w[l], ((0, 0), (0, LANE - N_EXPERTS)))
        wr_hi = wr.astype(BF16)
        wr_lo = (wr - wr_hi.astype(F32)).astype(BF16)
        xe, afft = _router(h, norm2_w[l][None, :], wr_hi, wr_lo)
        nb = p // LANE
        a3 = jnp.pad(afft.reshape(bsz, N_EXPERTS, nb, LANE), ((0, 0), (0, 0), (0, LANE - nb), (0, 0)),
                     constant_values=-1.0)
        idx = _select(afft, a3, strict_up, strict_lo, tri_lo, cap, cps)
        y = _ffn(idx.reshape(bsz, N_EXPERTS, 1, cps), xe, w_gate[l].astype(BF16), w_up[l].astype(BF16),
                 w_down[l].astype(BF16), cap, rows_ffn)

    return _final_add(h, y, seq)
```

```python
import functools

import jax
import jax.numpy as jnp
from jax import lax
from jax.experimental import pallas as pl
from jax.experimental.pallas import tpu as pltpu

F32 = jnp.float32
BF16 = jnp.bfloat16
I32 = jnp.int32

D_MODEL = 1024
N_META = 16
GRID_W = 64
ROPE_THETA = 10000.0
EPS = 1e-6
EXP_CLIP = 30.0
LOG2E = 1.4426950408889634

ATT_HEADS = 8
ATT_KV_HEADS = 2
HEAD_DIM = 64
ATT_WIDTH = ATT_HEADS * HEAD_DIM
KV_WIDTH = ATT_KV_HEADS * HEAD_DIM
ATT_REP = ATT_HEADS // ATT_KV_HEADS

HGRN_HEADS = 4
HGRN_DK = 128
HGRN_WIDTH = 512

SSM_HEADS = 8
SSM_HEAD_DIM = 64
SSM_GROUPS = 2
SSM_STATE = 64
SSM_CONV = 7
SSM_WIDTH = 512
SSM_CONV_DIM = 768
SSM_REP = SSM_HEADS // SSM_GROUPS

N_EXPERTS = 16
EXPERT_FF = 2048
CAPACITY_FACTOR = 2

LANE = 128
SUBLANE = 8
LEAD = 128
PADF = LEAD - N_META
NEG_BIG = -1e30
VMEM_LIMIT = 56 * 1024 * 1024
XE_WIDTH = D_MODEL + LANE
SCATTER_GROUP = 16
GLA_HEADS_PER_STEP = 4


def _cparams(sem):
    return pltpu.CompilerParams(dimension_semantics=sem, vmem_limit_bytes=VMEM_LIMIT)


def _dot(a, b):
    return jnp.dot(a, b, preferred_element_type=F32)


def _dot_nt(a, b):
    return lax.dot_general(a, b, (((1,), (1,)), ((), ())), preferred_element_type=F32)


def _split_bf16(x, n):
    parts, r = [], x
    for _ in range(n):
        p = r.astype(BF16)
        parts.append(p)
        r = r - p.astype(F32)
    return parts


def _dot_const_lhs(c, x, n=3):
    out = None
    for p in _split_bf16(x, n):
        t = _dot(c, p)
        out = t if out is None else out + t
    return out


def _dot_const_rhs(x, c, n=3):
    out = None
    for p in _split_bf16(x, n):
        t = _dot(p, c)
        out = t if out is None else out + t
    return out


def _silu(x):
    return x * jax.nn.sigmoid(x)


def _row_tile(p):
    return 384 if p % 384 == 0 else LANE


def _inproj_body(has_y, *refs):
    if has_y:
        h_ref, y_ref, nw_ref = refs[:3]
        ws = refs[3:8]
        hn_ref = refs[8]
        outs = refs[9:14]
        h = h_ref[...] + y_ref[...]
        hn_ref[...] = h
    else:
        h_ref, nw_ref = refs[:2]
        ws = refs[2:7]
        outs = refs[7:12]
        h = h_ref[...]
    ms = jnp.mean(h * h, axis=-1, keepdims=True)
    u = (h * lax.rsqrt(ms + EPS) * nw_ref[...]).astype(BF16)
    for w_ref, o_ref in zip(ws, outs):
        o_ref[...] = _dot(u, w_ref[...])


def _inproj(h2, y2, nw, ws):
    m = h2.shape[0]
    tm = 256 if m % 256 == 0 else LANE
    has_y = y2 is not None
    row = pl.BlockSpec((tm, D_MODEL), lambda i: (i, 0))
    const = lambda a: pl.BlockSpec(a.shape, lambda i: (0, 0))
    in_specs = [row] + ([row] if has_y else []) + [const(nw)] + [const(w) for w in ws]
    out_shape = ([jax.ShapeDtypeStruct((m, D_MODEL), F32)] if has_y else []) + [
        jax.ShapeDtypeStruct((m, w.shape[1]), F32) for w in ws]
    out_specs = ([row] if has_y else []) + [pl.BlockSpec((tm, w.shape[1]), lambda i: (i, 0)) for w in ws]
    args = [h2] + ([y2] if has_y else []) + [nw] + list(ws)
    res = pl.pallas_call(
        functools.partial(_inproj_body, has_y),
        grid=(m // tm,), in_specs=in_specs, out_specs=out_specs, out_shape=out_shape,
        compiler_params=_cparams(("parallel",)), name="inproj")(*args)
    if has_y:
        return res[0], res[1:]
    return h2, res


def _attn_prep_body(x_ref, cq_ref, sq_ref, ck_ref, sk_ref, qw_ref, kw_ref, gq_ref, gk_ref,
                    q_ref, kt_ref, v_ref):
    x = x_ref[0]
    q = x[:, :ATT_WIDTH]
    k = x[:, ATT_WIDTH:ATT_WIDTH + KV_WIDTH]
    v = x[:, ATT_WIDTH + KV_WIDTH:]

    def head_norm(a, g_ref, w_ref):
        ms = _dot_const_rhs(a * a, g_ref[...], 2)
        return a * lax.rsqrt(ms + EPS) * w_ref[...]

    def rope(a, c_ref, s_ref):
        width = a.shape[1]
        lane = lax.broadcasted_iota(I32, a.shape, 1)
        first = (lane & (HEAD_DIM - 1)) < HEAD_DIM // 2
        partner = jnp.where(first, pltpu.roll(a, width - HEAD_DIM // 2, 1),
                            pltpu.roll(a, HEAD_DIM // 2, 1))
        return a * c_ref[...] + partner * s_ref[...]

    q_ref[0] = rope(head_norm(q, gq_ref, qw_ref), cq_ref, sq_ref).astype(BF16)
    kr = rope(head_norm(k, gk_ref, kw_ref), ck_ref, sk_ref)
    kt_ref[0] = kr.T.astype(BF16)
    ones = jnp.ones((v.shape[0], HEAD_DIM), F32)
    for g in range(ATT_KV_HEADS):
        v_ref[0, g] = jnp.concatenate([v[:, g * HEAD_DIM:(g + 1) * HEAD_DIM], ones], axis=-1).astype(BF16)


def _attn_prep(att_p, tabs, qw, kw, gq, gk):
    b, p, _ = att_p.shape
    tp = _row_tile(p)
    cq, sq, ck, sk = tabs
    tab = lambda a: pl.BlockSpec((tp, a.shape[1]), lambda i, j: (j, 0))
    const = lambda a: pl.BlockSpec(a.shape, lambda i, j: (0, 0))
    return pl.pallas_call(
        _attn_prep_body, grid=(b, p // tp),
        in_specs=[pl.BlockSpec((1, tp, ATT_WIDTH + 2 * KV_WIDTH), lambda i, j: (i, j, 0)),
                  tab(cq), tab(sq), tab(ck), tab(sk), const(qw), const(kw), const(gq), const(gk)],
        out_specs=[pl.BlockSpec((1, tp, ATT_WIDTH), lambda i, j: (i, j, 0)),
                   pl.BlockSpec((1, KV_WIDTH, tp), lambda i, j: (i, 0, j)),
                   pl.BlockSpec((1, ATT_KV_HEADS, tp, 2 * HEAD_DIM), lambda i, j: (i, 0, j, 0))],
        out_shape=[jax.ShapeDtypeStruct((b, p, ATT_WIDTH), BF16),
                   jax.ShapeDtypeStruct((b, KV_WIDTH, p), BF16),
                   jax.ShapeDtypeStruct((b, ATT_KV_HEADS, p, 2 * HEAD_DIM), BF16)],
        compiler_params=_cparams(("parallel", "parallel")), name="attn_prep")(
            att_p, cq, sq, ck, sk, qw, kw, gq, gk)


def _attn_body(q_ref, kt_ref, v_ref, bias_ref, o_ref):
    kt0 = kt_ref[0, :, :LEAD]
    kt1 = kt_ref[0, :, LEAD:]
    v0 = v_ref[0, 0, :LEAD, :]
    v1 = v_ref[0, 0, LEAD:, :]
    bias = bias_ref[...]
    outs = []

    def scores(r):
        qh = q_ref[0, :, r * HEAD_DIM:(r + 1) * HEAD_DIM]
        return _dot(qh, kt0) + bias, _dot(qh, kt1)

    nxt = scores(0)
    for r in range(ATT_REP):
        s0, s1 = nxt
        if r + 1 < ATT_REP:
            nxt = scores(r + 1)
        m = jnp.maximum(jnp.max(s0, axis=-1, keepdims=True), jnp.max(s1, axis=-1, keepdims=True))
        p0 = jnp.exp(s0 - m).astype(BF16)
        p1 = jnp.exp(s1 - m).astype(BF16)
        ov = _dot(p0, v0) + _dot(p1, v1)
        outs.append(ov[:, :HEAD_DIM] / ov[:, HEAD_DIM:HEAD_DIM + 1])
    o_ref[0] = jnp.concatenate(outs, axis=-1)


def _attention(q, kt, v, bias):
    b, p, _ = q.shape
    tq = _row_tile(p)
    gw = ATT_REP * HEAD_DIM
    return pl.pallas_call(
        _attn_body, grid=(b, ATT_KV_HEADS, p // tq),
        in_specs=[pl.BlockSpec((1, tq, gw), lambda i, g, j: (i, j, g)),
                  pl.BlockSpec((1, HEAD_DIM, p), lambda i, g, j: (i, g, 0)),
                  pl.BlockSpec((1, 1, p, 2 * HEAD_DIM), lambda i, g, j: (i, g, 0, 0)),
                  pl.BlockSpec((1, LEAD), lambda i, g, j: (0, 0))],
        out_specs=pl.BlockSpec((1, tq, gw), lambda i, g, j: (i, j, g)),
        out_shape=jax.ShapeDtypeStruct((b, p, ATT_WIDTH), F32),
        compiler_params=_cparams(("parallel", "parallel", "parallel")), name="attention")(q, kt, v, bias)


def _gla_body(backward, finalize, *refs):
    if finalize:
        (q_ref, f_ref, v_ref, lb_ref, tri_ref, lvl_ref, gate_ref, of_ref, nw_ref,
         o_ref, st_ref, cum_ref) = refs
    else:
        q_ref, f_ref, v_ref, lb_ref, tri_ref, lvl_ref, o_ref, st_ref, cum_ref = refs
    hgrp = pl.program_id(1)
    c = pl.program_id(2)
    cc = pl.num_programs(2) - 1 - c if backward else c

    @pl.when(c == 0)
    def _():
        st_ref[...] = jnp.zeros_like(st_ref)

    n = LANE
    row = lax.broadcasted_iota(I32, (n, n), 0)
    valid = (cc * n + row) >= PADF
    r4 = row & 3
    odd = (row & 1) != 0
    lvl = lvl_ref[...]
    tri = tri_ref[...]

    for hd in range(GLA_HEADS_PER_STEP):
        sl = slice(hd * n, (hd + 1) * n)
        x = f_ref[0, :, sl]
        lb = lb_ref[hgrp * GLA_HEADS_PER_STEP + hd]
        g = jnp.minimum(x, 0.0) - jnp.log1p(jnp.exp(-jnp.abs(x)))
        g = g + jnp.log1p(lb * jnp.exp(jnp.minimum(-x, EXP_CLIP)))
        g = jnp.where(valid, jnp.minimum(g, 0.0), 0.0)
        k = jnp.where(valid, 1.0 - jnp.exp(g), 0.0)
        g = g * LOG2E
        q = _silu(q_ref[0, :, sl])
        v = v_ref[0, :, sl]

        cum = _dot_const_lhs(tri, g, 3)
        cum_ref[hd] = cum

        att = jnp.where(lvl == 0, _dot_nt(q.astype(BF16), k.astype(BF16)), 0.0)
        g_up = pltpu.roll(g, n - 1, 0)
        g_dn = pltpu.roll(g, 1, 0)
        s = 1
        while s < n:
            if s == 1:
                expo = jnp.where(odd, 0.0, g) if backward else jnp.where(odd, g, 0.0)
            elif s == 2:
                if backward:
                    expo = jnp.where(r4 == 0, g + g_up, jnp.where(r4 == 1, g, jnp.where(r4 == 2, 0.0, g_dn)))
                else:
                    expo = jnp.where(r4 == 0, g_up, jnp.where(r4 == 1, 0.0, jnp.where(r4 == 2, g, g + g_dn)))
            else:
                parts = []
                for blk in range(n // (2 * s)):
                    mid = blk * 2 * s + (s if backward else s - 1)
                    parts.append(jnp.broadcast_to(cum_ref[hd, pl.ds(mid, 1), :], (2 * s, n)))
                ref_cum = parts[0] if len(parts) == 1 else jnp.concatenate(parts, axis=0)
                expo = -jnp.abs(cum - ref_cum)
            e = jnp.exp2(expo)
            a = _dot_nt((q * e).astype(BF16), (k * e).astype(BF16))
            att = jnp.where(lvl == s, a, att)
            s *= 2

        edge = cum_ref[hd, pl.ds(0 if backward else n - 1, 1), :]
        st = st_ref[hd]
        o = _dot(att.astype(BF16), v.astype(BF16)) + _dot_nt((q * jnp.exp2(cum)).astype(BF16), st.astype(BF16))
        khat = k * jnp.exp2(edge - cum)
        st_ref[hd] = st * jnp.exp2(edge) + _dot(v.T.astype(BF16), khat.astype(BF16))

        if finalize:
            o = o + of_ref[0, :, sl]
            ms = jnp.mean(o * o, axis=-1, keepdims=True)
            o = o * lax.rsqrt(ms + EPS) * nw_ref[hgrp * GLA_HEADS_PER_STEP + hd]
            o = o * _silu(gate_ref[0, :, sl])
        o_ref[0, :, sl] = o


def _gla(hg, lb, tri, lvl, backward, o_fwd=None, nw=None):
    b, p, _ = hg.shape
    nc = p // LANE
    hp = GLA_HEADS_PER_STEP
    width = hp * LANE
    groups = HGRN_HEADS // hp
    finalize = o_fwd is not None
    cidx = (lambda c: nc - 1 - c) if backward else (lambda c: c)
    blk = lambda part: pl.BlockSpec((1, LANE, width), lambda i, h, c: (i, cidx(c), part * groups + h))
    per_head = pl.BlockSpec((HGRN_HEADS, 1, LANE), lambda i, h, c: (0, 0, 0))
    const = pl.BlockSpec((LANE, LANE), lambda i, h, c: (0, 0))
    in_specs = [blk(0), blk(2 if backward else 1), blk(3), per_head, const, const]
    args = [hg, hg, hg, lb, tri, lvl]
    if finalize:
        in_specs += [blk(4), blk(0), per_head]
        args += [hg, o_fwd, nw]
    return pl.pallas_call(
        functools.partial(_gla_body, backward, finalize), grid=(b, groups, nc),
        in_specs=in_specs, out_specs=blk(0),
        out_shape=jax.ShapeDtypeStruct((b, p, HGRN_WIDTH), F32),
        scratch_shapes=[pltpu.VMEM((hp, LANE, LANE), F32), pltpu.VMEM((hp, LANE, LANE), F32)],
        compiler_params=_cparams(("parallel", "parallel", "arbitrary")),
        name="gla_bwd" if backward else "gla_fwd")(*args)


def _conv_body(cur_ref, prev_ref, next_ref, cw_ref, cb_ref, xs_ref, bc_ref, ext_ref):
    j = pl.program_id(1)
    tp = cur_ref.shape[1]
    half = SSM_CONV // 2
    ext_ref[0:SUBLANE, :] = jnp.where(j > 0, prev_ref[0], 0.0)
    ext_ref[SUBLANE:SUBLANE + tp, :] = cur_ref[0]
    ext_ref[SUBLANE + tp:2 * SUBLANE + tp, :] = jnp.where(j < pl.num_programs(1) - 1, next_ref[0], 0.0)
    acc = jnp.broadcast_to(cb_ref[...], (tp, SSM_CONV_DIM))
    for w in range(SSM_CONV):
        acc = acc + ext_ref[pl.ds(SUBLANE - half + w, tp), :] * cw_ref[w:w + 1, :]
    y = _silu(acc)
    xs_ref[0] = y[:, :SSM_WIDTH]
    bc_ref[0] = y[:, SSM_WIDTH:]


def _ssd_conv(xbc, cw, cb):
    b, p, _ = xbc.shape
    tp = _row_tile(p)
    r8 = tp // SUBLANE
    last8 = p // SUBLANE - 1
    return pl.pallas_call(
        _conv_body, grid=(b, p // tp),
        in_specs=[pl.BlockSpec((1, tp, SSM_CONV_DIM), lambda i, j: (i, j, 0)),
                  pl.BlockSpec((1, SUBLANE, SSM_CONV_DIM), lambda i, j: (i, jnp.maximum(j * r8 - 1, 0), 0)),
                  pl.BlockSpec((1, SUBLANE, SSM_CONV_DIM), lambda i, j: (i, jnp.minimum((j + 1) * r8, last8), 0)),
                  pl.BlockSpec(cw.shape, lambda i, j: (0, 0)),
                  pl.BlockSpec(cb.shape, lambda i, j: (0, 0))],
        out_specs=[pl.BlockSpec((1, tp, SSM_WIDTH), lambda i, j: (i, j, 0)),
                   pl.BlockSpec((1, tp, 2 * SSM_GROUPS * SSM_STATE), lambda i, j: (i, j, 0))],
        out_shape=[jax.ShapeDtypeStruct((b, p, SSM_WIDTH), F32),
                   jax.ShapeDtypeStruct((b, p, 2 * SSM_GROUPS * SSM_STATE), F32)],
        scratch_shapes=[pltpu.VMEM((tp + 2 * SUBLANE, SSM_CONV_DIM), F32)],
        compiler_params=_cparams(("parallel", "parallel")), name="ssd_conv")(xbc, xbc, xbc, cw, cb)


def _ssd_body(backward, xs_ref, bc_ref, dt_ref, dtb_ref, a_ref, tri_ref, trit_ref, y_ref, st_ref):
    c = pl.program_id(1)
    cc = pl.num_programs(1) - 1 - c if backward else c

    @pl.when(c == 0)
    def _():
        st_ref[...] = jnp.zeros_like(st_ref)

    n = LANE
    row = lax.broadcasted_iota(I32, (n, n), 0)
    col = lax.broadcasted_iota(I32, (n, n), 1)
    valid = (cc * n + row) >= PADF
    tri = (col >= row) if backward else (col <= row)

    dt = jnp.where(valid, jax.nn.softplus(dt_ref[0] + dtb_ref[...]), 0.0)
    g = dt * a_ref[...]
    cum_col = _dot_const_lhs(tri_ref[...], g, 3)
    g_t = g.T
    dt_t = dt.T
    cum_row = _dot_const_rhs(g_t, trit_ref[...], 3)
    edge_t = 0 if backward else n - 1

    bc = bc_ref[0]
    b_t = bc[:, :SSM_GROUPS * SSM_STATE].T
    cm = bc[:, SSM_GROUPS * SSM_STATE:]
    xs = xs_ref[0]
    outs = []
    for grp in range(SSM_GROUPS):
        c_g = cm[:, grp * SSM_STATE:(grp + 1) * SSM_STATE].astype(BF16)
        bt_g = b_t[grp * SSM_STATE:(grp + 1) * SSM_STATE, :]
        cb = _dot(c_g, bt_g.astype(BF16))
        for hh in range(SSM_REP):
            head = grp * SSM_REP + hh
            ln = (SSM_HEADS if backward else 0) + head
            colv = cum_col[:, ln:ln + 1]
            rowv = cum_row[ln:ln + 1, :]
            dtrow = dt_t[ln:ln + 1, :]
            diff = colv - rowv
            decay = jnp.where(tri, jnp.exp(jnp.where(tri, diff, 0.0)), 0.0)
            att = cb * (decay * dtrow)
            x_h = xs[:, head * SSM_HEAD_DIM:(head + 1) * SSM_HEAD_DIM].astype(BF16)
            st = st_ref[head]
            o = _dot(att.astype(BF16), x_h) + _dot(c_g, st.astype(BF16)) * jnp.exp(colv)
            edge = rowv[:, edge_t:edge_t + 1]
            wrow = dtrow * jnp.exp(edge - rowv)
            st_ref[head] = jnp.exp(edge) * st + _dot((bt_g * wrow).astype(BF16), x_h)
            outs.append(o)
    y_ref[0] = jnp.concatenate(outs, axis=-1)


def _ssd(xs, bc, dtp, dtb, arow, tri, trit, backward):
    b, p, _ = xs.shape
    nc = p // LANE
    cidx = (lambda c: nc - 1 - c) if backward else (lambda c: c)
    blk = lambda w: pl.BlockSpec((1, LANE, w), lambda i, c: (i, cidx(c), 0))
    const = lambda a: pl.BlockSpec(a.shape, lambda i, c: (0, 0))
    return pl.pallas_call(
        functools.partial(_ssd_body, backward), grid=(b, nc),
        in_specs=[blk(SSM_WIDTH), blk(2 * SSM_GROUPS * SSM_STATE), blk(LANE),
                  const(dtb), const(arow), const(tri), const(trit)],
        out_specs=blk(SSM_WIDTH),
        out_shape=jax.ShapeDtypeStruct((b, p, SSM_WIDTH), F32),
        scratch_shapes=[pltpu.VMEM((SSM_HEADS, SSM_STATE, SSM_HEAD_DIM), F32)],
        compiler_params=_cparams(("parallel", "arbitrary")),
        name="ssd_bwd" if backward else "ssd_fwd")(xs, bc, dtp, dtb, arow, tri, trit)


def _outproj_body(att_ref, rec_ref, yf_ref, yb_ref, xs_ref, z_ref, h_ref, anw_ref, snw_ref, dsk_ref,
                  wa_ref, wr_ref, ws_ref, o_ref):
    tm = h_ref.shape[1]
    j = pl.program_id(1)

    def rms(a, w_ref):
        ms = jnp.mean(a * a, axis=-1, keepdims=True)
        return a * lax.rsqrt(ms + EPS) * w_ref[...]

    att = rms(att_ref[0], anw_ref)
    y = yf_ref[0] + yb_ref[0] + dsk_ref[...] * xs_ref[0]
    ssm = rms(y * _silu(z_ref[0]), snw_ref)
    out = (_dot(att.astype(BF16), wa_ref[...]) + _dot(rec_ref[0].astype(BF16), wr_ref[...])
           + _dot(ssm.astype(BF16), ws_ref[...]))
    rowi = j * tm + lax.broadcasted_iota(I32, (tm, 1), 0)
    o_ref[0] = jnp.where(rowi >= PADF, h_ref[0] + out, 0.0)


def _outproj(att, rec, yf, yb, xs, z, h, anw, snw, dsk, wa, wr, ws):
    b, p, _ = h.shape
    tm = _row_tile(p)
    blk = lambda w: pl.BlockSpec((1, tm, w), lambda i, j: (i, j, 0))
    const = lambda a: pl.BlockSpec(a.shape, lambda i, j: (0, 0))
    return pl.pallas_call(
        _outproj_body, grid=(b, p // tm),
        in_specs=[blk(512)] * 6 + [blk(D_MODEL), const(anw), const(snw), const(dsk),
                                   const(wa), const(wr), const(ws)],
        out_specs=blk(D_MODEL), out_shape=jax.ShapeDtypeStruct((b, p, D_MODEL), F32),
        compiler_params=_cparams(("parallel", "parallel")), name="outproj")(
            att, rec, yf, yb, xs, z, h, anw, snw, dsk, wa, wr, ws)


def _router_body(h_ref, nw_ref, wh_ref, wl_ref, xe_ref, afft_ref):
    tp = h_ref.shape[1]
    j = pl.program_id(1)
    h = h_ref[0]
    ms = jnp.mean(h * h, axis=-1, keepdims=True)
    xn = h * lax.rsqrt(ms + EPS) * nw_ref[...]
    x_hi = xn.astype(BF16)
    x_lo = (xn - x_hi.astype(F32)).astype(BF16)
    logits = _dot(x_hi, wh_ref[...]) + _dot(x_hi, wl_ref[...]) + _dot(x_lo, wh_ref[...])
    lane = lax.broadcasted_iota(I32, (tp, LANE), 1)
    real = lane < N_EXPERTS
    m = jnp.max(jnp.where(real, logits, NEG_BIG), axis=-1, keepdims=True)
    pexp = jnp.exp(jnp.where(real, logits - m, 0.0))
    aff = pexp / jnp.sum(jnp.where(real, pexp, 0.0), axis=-1, keepdims=True)
    aff = jnp.where(real, aff, 0.0)
    xe_ref[0, :, :D_MODEL] = x_hi.astype(F32)
    xe_ref[0, :, D_MODEL:] = aff
    rowi = j * tp + lax.broadcasted_iota(I32, (tp, LANE), 0)
    afft_ref[0] = jnp.where(rowi >= PADF, aff, -1.0).T[:N_EXPERTS, :]


def _router(h, nw, wh, wl):
    b, p, _ = h.shape
    tp = _row_tile(p)
    const = lambda a: pl.BlockSpec(a.shape, lambda i, j: (0, 0))
    return pl.pallas_call(
        _router_body, grid=(b, p // tp),
        in_specs=[pl.BlockSpec((1, tp, D_MODEL), lambda i, j: (i, j, 0)), const(nw), const(wh), const(wl)],
        out_specs=[pl.BlockSpec((1, tp, XE_WIDTH), lambda i, j: (i, j, 0)),
                   pl.BlockSpec((1, N_EXPERTS, tp), lambda i, j: (i, 0, j))],
        out_shape=[jax.ShapeDtypeStruct((b, p, XE_WIDTH), F32),
                   jax.ShapeDtypeStruct((b, N_EXPERTS, p), F32)],
        compiler_params=_cparams(("parallel", "parallel")), name="router")(h, nw, wh, wl)


def _select_body(cap, a2_ref, a3_ref, us_ref, ls_ref, li_ref, idx_ref, thr_ref, need_ref):
    n = LANE
    cps = idx_ref.shape[2]
    a2 = a2_ref[0]
    bits = lax.bitcast_convert_type(a2, I32)

    def search(i, t):
        cand = t | jnp.left_shift(jnp.int32(1), 30 - i)
        cnt = jnp.sum(jnp.where(bits >= cand, 1.0, 0.0), axis=1, keepdims=True)
        return jnp.where(cnt >= cap, cand, t)

    thr = lax.fori_loop(0, 31, search, jnp.zeros((N_EXPERTS, 1), I32))
    n_gt = jnp.sum(jnp.where(bits > thr, 1.0, 0.0), axis=1, keepdims=True)
    thr_ref[...] = jnp.broadcast_to(lax.bitcast_convert_type(thr, F32), (N_EXPERTS, n))
    need_ref[...] = jnp.broadcast_to(cap - n_gt, (N_EXPERTS, n))

    ones = jnp.ones((n, n), BF16)
    us = us_ref[...]
    ls = ls_ref[...]
    li = li_ref[...]
    c_row = lax.broadcasted_iota(I32, (1, cps), 1).astype(F32)
    c_full = lax.broadcasted_iota(I32, (n, cps), 1).astype(F32)
    k_full = lax.broadcasted_iota(I32, (n, cps), 0).astype(F32)
    reps = cps // n

    def per_expert(e, carry):
        a = a3_ref[0, e]
        vt = jnp.broadcast_to(thr_ref[pl.ds(e, 1), :], (n, n))
        nd = jnp.broadcast_to(need_ref[pl.ds(e, 1), :], (n, n))
        gt = jnp.where(a > vt, 1.0, 0.0)
        eq = jnp.where(a == vt, 1.0, 0.0).astype(BF16)
        eq_before = _dot(eq, us) + _dot(ls, _dot(eq, ones).astype(BF16))
        sel = gt + jnp.where(eq_before < nd, eq.astype(F32), 0.0)
        selb = sel.astype(BF16)
        tile_cnt = _dot(selb, ones)
        off_lo = _dot(ls, tile_cnt.astype(BF16))
        off_hi = off_lo + tile_cnt
        lo = jnp.concatenate([off_lo] * reps, axis=1)
        hi = jnp.concatenate([off_hi] * reps, axis=1)
        in_tile = jnp.where(lo <= c_full, jnp.where(c_full < hi, 1.0, 0.0), 0.0)
        incl = _dot(li, sel.T.astype(BF16))
        cnt_at = _dot(incl.astype(BF16), in_tile.astype(BF16))
        tile_of = jnp.sum(in_tile * k_full, axis=0, keepdims=True)
        rank = c_row - jnp.sum(in_tile * lo, axis=0, keepdims=True)
        local = jnp.sum(jnp.where(cnt_at <= rank, 1.0, 0.0), axis=0, keepdims=True)
        idx = jnp.where(c_row < cap, tile_of * n + local, 0.0)
        idx_ref[0, pl.ds(e, 1), :] = idx.astype(I32)
        return carry

    lax.fori_loop(0, N_EXPERTS, per_expert, 0)


def _select(afft, a3, us, ls, li, cap, cps):
    b, _, p = afft.shape
    const = lambda a: pl.BlockSpec(a.shape, lambda i: (0, 0))
    return pl.pallas_call(
        functools.partial(_select_body, cap), grid=(b,),
        in_specs=[pl.BlockSpec((1, N_EXPERTS, p), lambda i: (i, 0, 0)),
                  pl.BlockSpec((1, N_EXPERTS, LANE, LANE), lambda i: (i, 0, 0, 0)),
                  const(us), const(ls), const(li)],
        out_specs=pl.BlockSpec((1, N_EXPERTS, cps), lambda i: (i, 0, 0)),
        out_shape=jax.ShapeDtypeStruct((b, N_EXPERTS, cps), I32),
        scratch_shapes=[pltpu.VMEM((N_EXPERTS, LANE), F32), pltpu.VMEM((N_EXPERTS, LANE), F32)],
        compiler_params=_cparams(("parallel",)), name="select")(afft, a3, us, ls, li)


def _ffn_body(cap, idx_ref, xe_ref, wg_ref, wu_ref, wd_ref, y_ref, xs_ref, acc_ref, acc3_ref, sem):
    b = pl.program_id(0)
    e = pl.program_id(1)
    f = pl.program_id(2)
    nf = pl.num_programs(2)
    rows = xs_ref.shape[0]

    def row_copy(c, src_row):
        return pltpu.make_async_copy(xe_ref.at[b, pl.ds(src_row, 1), :], xs_ref.at[pl.ds(c, 1), :], sem)

    @pl.when((e == 0) & (f == 0))
    def _():
        y_ref[...] = jnp.zeros_like(y_ref)

    @pl.when(f == 0)
    def _():
        def issue(c, carry):
            row_copy(c, idx_ref[0, 0, 0, c]).start()
            return carry

        def drain(c, carry):
            row_copy(c, 0).wait()
            return carry

        lax.fori_loop(0, rows, issue, 0, unroll=8)
        lax.fori_loop(0, rows, drain, 0, unroll=8)

    x = xs_ref[:, :D_MODEL].astype(BF16)
    a = _dot(x, wg_ref[0])
    u = _dot(x, wu_ref[0])
    part = _dot((_silu(a) * u).astype(BF16), wd_ref[0])

    @pl.when(f == 0)
    def _():
        acc_ref[...] = part

    @pl.when(f > 0)
    def _():
        acc_ref[...] += part

    @pl.when(f == nf - 1)
    def _():
        lane = lax.broadcasted_iota(I32, (rows, LANE), 1)
        gate = jnp.sum(jnp.where(lane == e, xs_ref[:, D_MODEL:], 0.0), axis=-1, keepdims=True)
        rowi = lax.broadcasted_iota(I32, (rows, 1), 0)
        scaled = acc_ref[...] * jnp.where(rowi < cap, gate, 0.0)
        for a in range(D_MODEL // LANE):
            acc3_ref[pl.ds(a, rows, stride=SUBLANE), :] = scaled[:, a * LANE:(a + 1) * LANE]

        def scatter(grp, carry):
            base = grp * SCATTER_GROUP
            ts = [pl.multiple_of(idx_ref[0, 0, 0, base + u] * SUBLANE, SUBLANE) for u in range(SCATTER_GROUP)]
            vals = [y_ref[0, pl.ds(ts[u], SUBLANE), :]
                    + acc3_ref[pl.ds(pl.multiple_of((base + u) * SUBLANE, SUBLANE), SUBLANE), :]
                    for u in range(SCATTER_GROUP)]
            for u in range(SCATTER_GROUP):
                y_ref[0, pl.ds(ts[u], SUBLANE), :] = vals[u]
            return carry

        lax.fori_loop(0, rows // SCATTER_GROUP, scatter, 0)


def _ffn(idx4, xe, wg, wu, wd, cap, rows):
    b, p, _ = xe.shape
    tf = 512
    nf = EXPERT_FF // tf
    return pl.pallas_call(
        functools.partial(_ffn_body, cap), grid=(b, N_EXPERTS, nf),
        in_specs=[pl.BlockSpec((1, 1, 1, idx4.shape[3]), lambda i, e, f: (i, e, 0, 0),
                               memory_space=pltpu.SMEM),
                  pl.BlockSpec(memory_space=pl.ANY),
                  pl.BlockSpec((1, D_MODEL, tf), lambda i, e, f: (e, 0, f)),
                  pl.BlockSpec((1, D_MODEL, tf), lambda i, e, f: (e, 0, f)),
                  pl.BlockSpec((1, tf, D_MODEL), lambda i, e, f: (e, f, 0))],
        out_specs=pl.BlockSpec((1, p * SUBLANE, LANE), lambda i, e, f: (i, 0, 0)),
        out_shape=jax.ShapeDtypeStruct((b, p * SUBLANE, LANE), F32),
        scratch_shapes=[pltpu.VMEM((rows, XE_WIDTH), F32), pltpu.VMEM((rows, D_MODEL), F32),
                        pltpu.VMEM((rows * SUBLANE, LANE), F32), pltpu.SemaphoreType.DMA(())],
        compiler_params=_cparams(("parallel", "arbitrary", "arbitrary")), name="expert_ffn")(
            idx4, xe, wg, wu, wd).reshape(b, p, D_MODEL)


def _final_body(h_ref, y_ref, o_ref):
    o_ref[...] = h_ref[...] + y_ref[...]


def _final_add(h, y, seq):
    b, p, _ = h.shape
    lead_blocks = LEAD // LANE
    src = pl.BlockSpec((1, LANE, D_MODEL), lambda i, j: (i, j + lead_blocks, 0))
    return pl.pallas_call(
        _final_body, grid=(b, seq // LANE), in_specs=[src, src],
        out_specs=pl.BlockSpec((1, LANE, D_MODEL), lambda i, j: (i, j, 0)),
        out_shape=jax.ShapeDtypeStruct((b, seq, D_MODEL), F32),
        compiler_params=_cparams(("parallel", "parallel")), name="final_add")(h, y)


def _rope_tables(seq):
    rows = seq // GRID_W
    row = jnp.repeat(jnp.arange(rows), GRID_W).astype(F32)
    col = (jnp.arange(rows * GRID_W) % GRID_W).astype(F32)
    n_pair = HEAD_DIM // 4
    inv = ROPE_THETA ** (-jnp.arange(n_pair, dtype=F32) / n_pair)
    ang = jnp.concatenate([row[:, None] * inv, col[:, None] * inv], axis=-1)
    ang = jnp.concatenate([jnp.zeros((LEAD, HEAD_DIM // 2), F32), ang], axis=0)
    cos, sin = jnp.cos(ang), jnp.sin(ang)
    cos64 = jnp.concatenate([cos, cos], axis=-1)
    sin64 = jnp.concatenate([-sin, sin], axis=-1)
    scale = HEAD_DIM ** -0.5
    return (jnp.tile(cos64, (1, ATT_HEADS)) * scale, jnp.tile(sin64, (1, ATT_HEADS)) * scale,
            jnp.tile(cos64, (1, ATT_KV_HEADS)), jnp.tile(sin64, (1, ATT_KV_HEADS)))


def _block_mean_matrix(width):
    i = jnp.arange(width)
    return jnp.where((i[:, None] // HEAD_DIM) == (i[None, :] // HEAD_DIM), 1.0 / HEAD_DIM, 0.0).astype(BF16)


def kernel(x, meta_tokens, norm1_w, w_in, q_norm_w, k_norm_w, attn_norm_w, hgrn_lb, hgrn_norm_w, conv_w, conv_b, dt_bias, a_log, d_skip, ssm_norm_w, w_out, norm2_w, router_w, w_gate, w_up, w_down):
    bsz, seq, _ = x.shape
    depth = norm1_w.shape[0]
    assert seq % LANE == 0 and seq % GRID_W == 0 and D_MODEL == SUBLANE * LANE
    p = LEAD + seq
    assert p // LANE <= LANE
    n_tok = N_META + seq
    cap = CAPACITY_FACTOR * n_tok // N_EXPERTS
    rows_ffn = -(-cap // 16) * 16
    cps = -(-cap // LANE) * LANE

    h = jnp.concatenate([jnp.zeros((bsz, PADF, D_MODEL), x.dtype),
                         jnp.broadcast_to(meta_tokens[None].astype(x.dtype), (bsz, N_META, D_MODEL)), x], axis=1)

    tabs = _rope_tables(seq)
    tok = jnp.arange(p)
    att_bias = jnp.where(tok[:LEAD] >= PADF, 0.0, NEG_BIG).astype(F32)[None, :]
    gq = _block_mean_matrix(ATT_WIDTH)
    gk = _block_mean_matrix(KV_WIDTH)
    ii = jnp.arange(LANE)
    tri_lo = (ii[None, :] <= ii[:, None]).astype(BF16)
    tri_up = (ii[None, :] >= ii[:, None]).astype(BF16)
    strict_up = (ii[:, None] < ii[None, :]).astype(BF16)
    strict_lo = (ii[None, :] < ii[:, None]).astype(BF16)

    xor = ii[:, None] ^ ii[None, :]
    msb = jnp.zeros_like(xor)
    for bit in range(7):
        msb = jnp.where(xor >= (1 << bit), 1 << bit, msb)
    lvl_f = jnp.where(ii[:, None] >= ii[None, :], msb, -1).astype(I32)
    lvl_b = jnp.where(ii[:, None] <= ii[None, :], msb, -1).astype(I32)

    soft = jax.nn.softmax(hgrn_lb.astype(F32), axis=1)
    lower_bounds = jnp.clip(jnp.cumsum(soft, axis=1) - soft[:, :1], 0.0, 1.0)

    y = None
    for l in range(depth):
        w = w_in[l]
        o = 0
        pieces = []
        for width in (ATT_WIDTH + 2 * KV_WIDTH, 5 * HGRN_WIDTH, SSM_WIDTH, SSM_CONV_DIM, 2 * SSM_HEADS):
            pieces.append(w[:, o:o + width])
            o += width
        pieces[4] = jnp.pad(pieces[4], ((0, 0), (0, LANE - 2 * SSM_HEADS)))
        ws = [a.astype(BF16) for a in pieces]
        h2, (att_p, hg_p, z_p, xbc_p, dt_p) = _inproj(
            h.reshape(bsz * p, D_MODEL), None if y is None else y.reshape(bsz * p, D_MODEL),
            norm1_w[l][None, :], ws)
        h = h2.reshape(bsz, p, D_MODEL)
        r3 = lambda a: a.reshape(bsz, p, a.shape[-1])
        att_p, hg_p, z_p, xbc_p, dt_p = r3(att_p), r3(hg_p), r3(z_p), r3(xbc_p), r3(dt_p)

        qr, kt, vv = _attn_prep(att_p, tabs, jnp.tile(q_norm_w[l], ATT_HEADS)[None, :],
                                jnp.tile(k_norm_w[l], ATT_KV_HEADS)[None, :], gq, gk)
        att = _attention(qr, kt, vv, att_bias)

        lb_f = lower_bounds[0, l].reshape(HGRN_HEADS, 1, HGRN_DK)
        lb_b = lower_bounds[1, l].reshape(HGRN_HEADS, 1, HGRN_DK)
        o_f = _gla(hg_p, lb_f, tri_lo, lvl_f, False)
        rec = _gla(hg_p, lb_b, tri_up, lvl_b, True, o_f, hgrn_norm_w[l].reshape(HGRN_HEADS, 1, HGRN_DK))

        cw = jnp.pad(conv_w[l].T, ((0, SUBLANE - SSM_CONV), (0, 0)))
        xs, bc = _ssd_conv(xbc_p, cw, conv_b[l][None, :])
        dtb = jnp.pad(dt_bias[l].reshape(-1), (0, LANE - 2 * SSM_HEADS))[None, :]
        arow = jnp.pad(-jnp.exp(a_log[l].astype(F32)).reshape(-1), (0, LANE - 2 * SSM_HEADS))[None, :]
        y_f = _ssd(xs, bc, dt_p, dtb, arow, tri_lo, tri_up, False)
        y_b = _ssd(xs, bc, dt_p, dtb, arow, tri_up, tri_lo, True)

        wo = w_out[l].astype(BF16)
        h = _outproj(att, rec, y_f, y_b, xs, z_p, h, attn_norm_w[l][None, :], ssm_norm_w[l][None, :],
                     jnp.repeat(d_skip[l], SSM_HEAD_DIM)[None, :],
                     wo[:ATT_WIDTH], wo[ATT_WIDTH:ATT_WIDTH + HGRN_WIDTH], wo[ATT_WIDTH + HGRN_WIDTH:])

        wr = jnp.pad(router_w[l], ((0, 0), (0, LANE - N_EXPERTS)))
        wr_hi = wr.astype(BF16)
        wr_lo = (wr - wr_hi.astype(F32)).astype(BF16)
        xe, afft = _router(h, norm2_w[l][None, :], wr_hi, wr_lo)
        nb = p // LANE
        a3 = jnp.pad(afft.reshape(bsz, N_EXPERTS, nb, LANE), ((0, 0), (0, 0), (0, LANE - nb), (0, 0)),
                     constant_values=-1.0)
        idx = _select(afft, a3, strict_up, strict_lo, tri_lo, cap, cps)
        y = _ffn(idx.reshape(bsz, N_EXPERTS, 1, cps), xe, w_gate[l].astype(BF16), w_up[l].astype(BF16),
                 w_down[l].astype(BF16), cap, rows_ffn)

    return _final_add(h, y, seq)
```

```python
import functools

import jax
import jax.numpy as jnp
from jax import lax
from jax.experimental import pallas as pl
from jax.experimental.pallas import tpu as pltpu

F32 = jnp.float32
BF16 = jnp.bfloat16
I32 = jnp.int32

D_MODEL = 1024
N_META = 16
GRID_W = 64
ROPE_THETA = 10000.0
EPS = 1e-6
EXP_CLIP = 30.0
LOG2E = 1.4426950408889634

ATT_HEADS = 8
ATT_KV_HEADS = 2
HEAD_DIM = 64
ATT_WIDTH = ATT_HEADS * HEAD_DIM
KV_WIDTH = ATT_KV_HEADS * HEAD_DIM
ATT_REP = ATT_HEADS // ATT_KV_HEADS

HGRN_HEADS = 4
HGRN_DK = 128
HGRN_WIDTH = 512

SSM_HEADS = 8
SSM_HEAD_DIM = 64
SSM_GROUPS = 2
SSM_STATE = 64
SSM_CONV = 7
SSM_WIDTH = 512
SSM_CONV_DIM = 768
SSM_REP = SSM_HEADS // SSM_GROUPS

N_EXPERTS = 16
EXPERT_FF = 2048
CAPACITY_FACTOR = 2

LANE = 128
SUBLANE = 8
LEAD = 128
PADF = LEAD - N_META
NEG_BIG = -1e30
VMEM_LIMIT = 56 * 1024 * 1024
SCATTER_GROUP = 16
GLA_HEADS_PER_STEP = 4


def _cparams(sem):
    return pltpu.CompilerParams(dimension_semantics=sem, vmem_limit_bytes=VMEM_LIMIT)


def _dot(a, b):
    return jnp.dot(a, b, preferred_element_type=F32)


def _dot_nt(a, b):
    return lax.dot_general(a, b, (((1,), (1,)), ((), ())), preferred_element_type=F32)


def _split_bf16(x, n):
    parts, r = [], x
    for _ in range(n):
        p = r.astype(BF16)
        parts.append(p)
        r = r - p.astype(F32)
    return parts


def _dot_const_lhs(c, x, n=3):
    out = None
    for p in _split_bf16(x, n):
        t = _dot(c, p)
        out = t if out is None else out + t
    return out


def _dot_const_rhs(x, c, n=3):
    out = None
    for p in _split_bf16(x, n):
        t = _dot(p, c)
        out = t if out is None else out + t
    return out


def _silu(x):
    return x * jax.nn.sigmoid(x)


def _row_tile(p):
    return 384 if p % 384 == 0 else LANE


def _inproj_body(has_y, *refs):
    if has_y:
        h_ref, y_ref, nw_ref = refs[:3]
        ws = refs[3:8]
        hn_ref = refs[8]
        outs = refs[9:14]
        h = h_ref[...] + y_ref[...]
        hn_ref[...] = h
    else:
        h_ref, nw_ref = refs[:2]
        ws = refs[2:7]
        outs = refs[7:12]
        h = h_ref[...]
    ms = jnp.mean(h * h, axis=-1, keepdims=True)
    u = (h * lax.rsqrt(ms + EPS) * nw_ref[...]).astype(BF16)
    for w_ref, o_ref in zip(ws, outs):
        o_ref[...] = _dot(u, w_ref[...])


def _inproj(h2, y2, nw, ws):
    m = h2.shape[0]
    tm = 256 if m % 256 == 0 else LANE
    has_y = y2 is not None
    row = pl.BlockSpec((tm, D_MODEL), lambda i: (i, 0))
    const = lambda a: pl.BlockSpec(a.shape, lambda i: (0, 0))
    in_specs = [row] + ([row] if has_y else []) + [const(nw)] + [const(w) for w in ws]
    out_shape = ([jax.ShapeDtypeStruct((m, D_MODEL), F32)] if has_y else []) + [
        jax.ShapeDtypeStruct((m, w.shape[1]), F32) for w in ws]
    out_specs = ([row] if has_y else []) + [pl.BlockSpec((tm, w.shape[1]), lambda i: (i, 0)) for w in ws]
    args = [h2] + ([y2] if has_y else []) + [nw] + list(ws)
    res = pl.pallas_call(
        functools.partial(_inproj_body, has_y),
        grid=(m // tm,), in_specs=in_specs, out_specs=out_specs, out_shape=out_shape,
        compiler_params=_cparams(("parallel",)), name="inproj")(*args)
    if has_y:
        return res[0], res[1:]
    return h2, res


def _attn_prep_body(x_ref, cq_ref, sq_ref, ck_ref, sk_ref, qw_ref, kw_ref, gq_ref, gk_ref,
                    q_ref, kt_ref, v_ref):
    x = x_ref[0]
    q = x[:, :ATT_WIDTH]
    k = x[:, ATT_WIDTH:ATT_WIDTH + KV_WIDTH]
    v = x[:, ATT_WIDTH + KV_WIDTH:]

    def head_norm(a, g_ref, w_ref):
        ms = _dot_const_rhs(a * a, g_ref[...], 2)
        return a * lax.rsqrt(ms + EPS) * w_ref[...]

    def rope(a, c_ref, s_ref):
        width = a.shape[1]
        lane = lax.broadcasted_iota(I32, a.shape, 1)
        first = (lane & (HEAD_DIM - 1)) < HEAD_DIM // 2
        partner = jnp.where(first, pltpu.roll(a, width - HEAD_DIM // 2, 1),
                            pltpu.roll(a, HEAD_DIM // 2, 1))
        return a * c_ref[...] + partner * s_ref[...]

    q_ref[0] = rope(head_norm(q, gq_ref, qw_ref), cq_ref, sq_ref).astype(BF16)
    kr = rope(head_norm(k, gk_ref, kw_ref), ck_ref, sk_ref)
    kt_ref[0] = kr.T.astype(BF16)
    ones = jnp.ones((v.shape[0], HEAD_DIM), F32)
    for g in range(ATT_KV_HEADS):
        v_ref[0, g] = jnp.concatenate([v[:, g * HEAD_DIM:(g + 1) * HEAD_DIM], ones], axis=-1).astype(BF16)


def _attn_prep(att_p, tabs, qw, kw, gq, gk):
    b, p, _ = att_p.shape
    tp = _row_tile(p)
    cq, sq, ck, sk = tabs
    tab = lambda a: pl.BlockSpec((tp, a.shape[1]), lambda i, j: (j, 0))
    const = lambda a: pl.BlockSpec(a.shape, lambda i, j: (0, 0))
    return pl.pallas_call(
        _attn_prep_body, grid=(b, p // tp),
        in_specs=[pl.BlockSpec((1, tp, ATT_WIDTH + 2 * KV_WIDTH), lambda i, j: (i, j, 0)),
                  tab(cq), tab(sq), tab(ck), tab(sk), const(qw), const(kw), const(gq), const(gk)],
        out_specs=[pl.BlockSpec((1, tp, ATT_WIDTH), lambda i, j: (i, j, 0)),
                   pl.BlockSpec((1, KV_WIDTH, tp), lambda i, j: (i, 0, j)),
                   pl.BlockSpec((1, ATT_KV_HEADS, tp, 2 * HEAD_DIM), lambda i, j: (i, 0, j, 0))],
        out_shape=[jax.ShapeDtypeStruct((b, p, ATT_WIDTH), BF16),
                   jax.ShapeDtypeStruct((b, KV_WIDTH, p), BF16),
                   jax.ShapeDtypeStruct((b, ATT_KV_HEADS, p, 2 * HEAD_DIM), BF16)],
        compiler_params=_cparams(("parallel", "parallel")), name="attn_prep")(
            att_p, cq, sq, ck, sk, qw, kw, gq, gk)


def _attn_body(q_ref, kt_ref, v_ref, bias_ref, o_ref):
    kt0 = kt_ref[0, :, :LEAD]
    kt1 = kt_ref[0, :, LEAD:]
    v0 = v_ref[0, 0, :LEAD, :]
    v1 = v_ref[0, 0, LEAD:, :]
    bias = bias_ref[...]
    outs = []

    def scores(r):
        qh = q_ref[0, :, r * HEAD_DIM:(r + 1) * HEAD_DIM]
        return _dot(qh, kt0) + bias, _dot(qh, kt1)

    nxt = scores(0)
    for r in range(ATT_REP):
        s0, s1 = nxt
        if r + 1 < ATT_REP:
            nxt = scores(r + 1)
        m = jnp.maximum(jnp.max(s0, axis=-1, keepdims=True), jnp.max(s1, axis=-1, keepdims=True))
        p0 = jnp.exp(s0 - m).astype(BF16)
        p1 = jnp.exp(s1 - m).astype(BF16)
        ov = _dot(p0, v0) + _dot(p1, v1)
        outs.append(ov[:, :HEAD_DIM] / ov[:, HEAD_DIM:HEAD_DIM + 1])
    o_ref[0] = jnp.concatenate(outs, axis=-1)


def _attention(q, kt, v, bias):
    b, p, _ = q.shape
    tq = _row_tile(p)
    gw = ATT_REP * HEAD_DIM
    return pl.pallas_call(
        _attn_body, grid=(b, ATT_KV_HEADS, p // tq),
        in_specs=[pl.BlockSpec((1, tq, gw), lambda i, g, j: (i, j, g)),
                  pl.BlockSpec((1, HEAD_DIM, p), lambda i, g, j: (i, g, 0)),
                  pl.BlockSpec((1, 1, p, 2 * HEAD_DIM), lambda i, g, j: (i, g, 0, 0)),
                  pl.BlockSpec((1, LEAD), lambda i, g, j: (0, 0))],
        out_specs=pl.BlockSpec((1, tq, gw), lambda i, g, j: (i, j, g)),
        out_shape=jax.ShapeDtypeStruct((b, p, ATT_WIDTH), F32),
        compiler_params=_cparams(("parallel", "parallel", "parallel")), name="attention")(q, kt, v, bias)


def _gla_body(backward, finalize, *refs):
    if finalize:
        (q_ref, f_ref, v_ref, lb_ref, tri_ref, lvl_ref, gate_ref, of_ref, nw_ref,
         o_ref, st_ref, cum_ref) = refs
    else:
        q_ref, f_ref, v_ref, lb_ref, tri_ref, lvl_ref, o_ref, st_ref, cum_ref = refs
    hgrp = pl.program_id(1)
    c = pl.program_id(2)
    cc = pl.num_programs(2) - 1 - c if backward else c

    @pl.when(c == 0)
    def _():
        st_ref[...] = jnp.zeros_like(st_ref)

    n = LANE
    row = lax.broadcasted_iota(I32, (n, n), 0)
    valid = (cc * n + row) >= PADF
    r4 = row & 3
    odd = (row & 1) != 0
    lvl = lvl_ref[...]
    tri = tri_ref[...]

    for hd in range(GLA_HEADS_PER_STEP):
        sl = slice(hd * n, (hd + 1) * n)
        x = f_ref[0, :, sl]
        lb = lb_ref[hgrp * GLA_HEADS_PER_STEP + hd]
        g = jnp.minimum(x, 0.0) - jnp.log1p(jnp.exp(-jnp.abs(x)))
        g = g + jnp.log1p(lb * jnp.exp(jnp.minimum(-x, EXP_CLIP)))
        g = jnp.where(valid, jnp.minimum(g, 0.0), 0.0)
        k = jnp.where(valid, 1.0 - jnp.exp(g), 0.0)
        g = g * LOG2E
        q = _silu(q_ref[0, :, sl])
        v = v_ref[0, :, sl]

        cum = _dot_const_lhs(tri, g, 3)
        cum_ref[hd] = cum

        q_bf = q.astype(BF16)
        k_bf = k.astype(BF16)
        att = jnp.where(lvl == 0, _dot_nt(q_bf, k_bf), 0.0)
        g_up = pltpu.roll(g, n - 1, 0)
        g_dn = pltpu.roll(g, 1, 0)
        s = 1
        while s < n:
            if s == 1:
                expo = jnp.where(odd, 0.0, g) if backward else jnp.where(odd, g, 0.0)
            elif s == 2:
                if backward:
                    expo = jnp.where(r4 == 0, g + g_up, jnp.where(r4 == 1, g, jnp.where(r4 == 2, 0.0, g_dn)))
                else:
                    expo = jnp.where(r4 == 0, g_up, jnp.where(r4 == 1, 0.0, jnp.where(r4 == 2, g, g + g_dn)))
            else:
                parts = []
                for blk in range(n // (2 * s)):
                    mid = blk * 2 * s + (s if backward else s - 1)
                    parts.append(jnp.broadcast_to(cum_ref[hd, pl.ds(mid, 1), :], (2 * s, n)))
                ref_cum = parts[0] if len(parts) == 1 else jnp.concatenate(parts, axis=0)
                expo = -jnp.abs(cum - ref_cum)
            e = jnp.exp2(expo).astype(BF16)
            a = _dot_nt(q_bf * e, k_bf * e)
            att = jnp.where(lvl == s, a, att)
            s *= 2

        edge = cum_ref[hd, pl.ds(0 if backward else n - 1, 1), :]
        st = st_ref[hd]
        o = _dot(att.astype(BF16), v.astype(BF16)) + _dot_nt((q * jnp.exp2(cum)).astype(BF16), st.astype(BF16))
        khat = k * jnp.exp2(edge - cum)
        st_ref[hd] = st * jnp.exp2(edge) + _dot(v.T.astype(BF16), khat.astype(BF16))

        if finalize:
            o = o + of_ref[0, :, sl]
            ms = jnp.mean(o * o, axis=-1, keepdims=True)
            o = o * lax.rsqrt(ms + EPS) * nw_ref[hgrp * GLA_HEADS_PER_STEP + hd]
            o = o * _silu(gate_ref[0, :, sl])
        o_ref[0, :, sl] = o


def _gla(hg, lb, tri, lvl, backward, o_fwd=None, nw=None):
    b, p, _ = hg.shape
    nc = p // LANE
    hp = GLA_HEADS_PER_STEP
    width = hp * LANE
    groups = HGRN_HEADS // hp
    finalize = o_fwd is not None
    cidx = (lambda c: nc - 1 - c) if backward else (lambda c: c)
    blk = lambda part: pl.BlockSpec((1, LANE, width), lambda i, h, c: (i, cidx(c), part * groups + h))
    per_head = pl.BlockSpec((HGRN_HEADS, 1, LANE), lambda i, h, c: (0, 0, 0))
    const = pl.BlockSpec((LANE, LANE), lambda i, h, c: (0, 0))
    in_specs = [blk(0), blk(2 if backward else 1), blk(3), per_head, const, const]
    args = [hg, hg, hg, lb, tri, lvl]
    if finalize:
        in_specs += [blk(4), blk(0), per_head]
        args += [hg, o_fwd, nw]
    return pl.pallas_call(
        functools.partial(_gla_body, backward, finalize), grid=(b, groups, nc),
        in_specs=in_specs, out_specs=blk(0),
        out_shape=jax.ShapeDtypeStruct((b, p, HGRN_WIDTH), F32),
        scratch_shapes=[pltpu.VMEM((hp, LANE, LANE), F32), pltpu.VMEM((hp, LANE, LANE), F32)],
        compiler_params=_cparams(("parallel", "parallel", "arbitrary")),
        name="gla_bwd" if backward else "gla_fwd")(*args)


def _conv_body(cur_ref, prev_ref, next_ref, cw_ref, cb_ref, xs_ref, bc_ref, ext_ref):
    j = pl.program_id(1)
    tp = cur_ref.shape[1]
    half = SSM_CONV // 2
    ext_ref[0:SUBLANE, :] = jnp.where(j > 0, prev_ref[0], 0.0)
    ext_ref[SUBLANE:SUBLANE + tp, :] = cur_ref[0]
    ext_ref[SUBLANE + tp:2 * SUBLANE + tp, :] = jnp.where(j < pl.num_programs(1) - 1, next_ref[0], 0.0)
    acc = jnp.broadcast_to(cb_ref[...], (tp, SSM_CONV_DIM))
    for w in range(SSM_CONV):
        acc = acc + ext_ref[pl.ds(SUBLANE - half + w, tp), :] * cw_ref[w:w + 1, :]
    y = _silu(acc)
    xs_ref[0] = y[:, :SSM_WIDTH]
    bc_ref[0] = y[:, SSM_WIDTH:]


def _ssd_conv(xbc, cw, cb):
    b, p, _ = xbc.shape
    tp = _row_tile(p)
    r8 = tp // SUBLANE
    last8 = p // SUBLANE - 1
    return pl.pallas_call(
        _conv_body, grid=(b, p // tp),
        in_specs=[pl.BlockSpec((1, tp, SSM_CONV_DIM), lambda i, j: (i, j, 0)),
                  pl.BlockSpec((1, SUBLANE, SSM_CONV_DIM), lambda i, j: (i, jnp.maximum(j * r8 - 1, 0), 0)),
                  pl.BlockSpec((1, SUBLANE, SSM_CONV_DIM), lambda i, j: (i, jnp.minimum((j + 1) * r8, last8), 0)),
                  pl.BlockSpec(cw.shape, lambda i, j: (0, 0)),
                  pl.BlockSpec(cb.shape, lambda i, j: (0, 0))],
        out_specs=[pl.BlockSpec((1, tp, SSM_WIDTH), lambda i, j: (i, j, 0)),
                   pl.BlockSpec((1, tp, 2 * SSM_GROUPS * SSM_STATE), lambda i, j: (i, j, 0))],
        out_shape=[jax.ShapeDtypeStruct((b, p, SSM_WIDTH), F32),
                   jax.ShapeDtypeStruct((b, p, 2 * SSM_GROUPS * SSM_STATE), F32)],
        scratch_shapes=[pltpu.VMEM((tp + 2 * SUBLANE, SSM_CONV_DIM), F32)],
        compiler_params=_cparams(("parallel", "parallel")), name="ssd_conv")(xbc, xbc, xbc, cw, cb)


def _ssd_body(backward, xs_ref, bc_ref, dt_ref, dtb_ref, a_ref, tri_ref, trit_ref, y_ref, st_ref):
    c = pl.program_id(1)
    cc = pl.num_programs(1) - 1 - c if backward else c

    @pl.when(c == 0)
    def _():
        st_ref[...] = jnp.zeros_like(st_ref)

    n = LANE
    row = lax.broadcasted_iota(I32, (n, n), 0)
    col = lax.broadcasted_iota(I32, (n, n), 1)
    valid = (cc * n + row) >= PADF
    tri = (col >= row) if backward else (col <= row)

    dt = jnp.where(valid, jax.nn.softplus(dt_ref[0] + dtb_ref[...]), 0.0)
    g = dt * a_ref[...]
    cum_col = _dot_const_lhs(tri_ref[...], g, 3)
    g_t = g.T
    dt_t = dt.T
    cum_row = _dot_const_rhs(g_t, trit_ref[...], 3)
    edge_t = 0 if backward else n - 1

    bc = bc_ref[0]
    b_t = bc[:, :SSM_GROUPS * SSM_STATE].T
    cm = bc[:, SSM_GROUPS * SSM_STATE:]
    xs = xs_ref[0]
    outs = []
    for grp in range(SSM_GROUPS):
        c_g = cm[:, grp * SSM_STATE:(grp + 1) * SSM_STATE].astype(BF16)
        bt_g = b_t[grp * SSM_STATE:(grp + 1) * SSM_STATE, :]
        cb = _dot(c_g, bt_g.astype(BF16))
        for hh in range(SSM_REP):
            head = grp * SSM_REP + hh
            ln = (SSM_HEADS if backward else 0) + head
            colv = cum_col[:, ln:ln + 1]
            rowv = cum_row[ln:ln + 1, :]
            dtrow = dt_t[ln:ln + 1, :]
            diff = colv - rowv
            decay = jnp.where(tri, jnp.exp(jnp.where(tri, diff, 0.0)), 0.0)
            att = cb * (decay * dtrow)
            x_h = xs[:, head * SSM_HEAD_DIM:(head + 1) * SSM_HEAD_DIM].astype(BF16)
            st = st_ref[head]
            o = _dot(att.astype(BF16), x_h) + _dot(c_g, st.astype(BF16)) * jnp.exp(colv)
            edge = rowv[:, edge_t:edge_t + 1]
            wrow = dtrow * jnp.exp(edge - rowv)
            st_ref[head] = jnp.exp(edge) * st + _dot((bt_g * wrow).astype(BF16), x_h)
            outs.append(o)
    y_ref[0] = jnp.concatenate(outs, axis=-1)


def _ssd(xs, bc, dtp, dtb, arow, tri, trit, backward):
    b, p, _ = xs.shape
    nc = p // LANE
    cidx = (lambda c: nc - 1 - c) if backward else (lambda c: c)
    blk = lambda w: pl.BlockSpec((1, LANE, w), lambda i, c: (i, cidx(c), 0))
    const = lambda a: pl.BlockSpec(a.shape, lambda i, c: (0, 0))
    return pl.pallas_call(
        functools.partial(_ssd_body, backward), grid=(b, nc),
        in_specs=[blk(SSM_WIDTH), blk(2 * SSM_GROUPS * SSM_STATE), blk(LANE),
                  const(dtb), const(arow), const(tri), const(trit)],
        out_specs=blk(SSM_WIDTH),
        out_shape=jax.ShapeDtypeStruct((b, p, SSM_WIDTH), F32),
        scratch_shapes=[pltpu.VMEM((SSM_HEADS, SSM_STATE, SSM_HEAD_DIM), F32)],
        compiler_params=_cparams(("parallel", "arbitrary")),
        name="ssd_bwd" if backward else "ssd_fwd")(xs, bc, dtp, dtb, arow, tri, trit)


def _outproj_body(att_ref, rec_ref, yf_ref, yb_ref, xs_ref, z_ref, h_ref, anw_ref, snw_ref, dsk_ref,
                  wa_ref, wr_ref, ws_ref, o_ref):
    tm = h_ref.shape[1]
    j = pl.program_id(1)

    def rms(a, w_ref):
        ms = jnp.mean(a * a, axis=-1, keepdims=True)
        return a * lax.rsqrt(ms + EPS) * w_ref[...]

    att = rms(att_ref[0], anw_ref)
    y = yf_ref[0] + yb_ref[0] + dsk_ref[...] * xs_ref[0]
    ssm = rms(y * _silu(z_ref[0]), snw_ref)
    out = (_dot(att.astype(BF16), wa_ref[...]) + _dot(rec_ref[0].astype(BF16), wr_ref[...])
           + _dot(ssm.astype(BF16), ws_ref[...]))
    rowi = j * tm + lax.broadcasted_iota(I32, (tm, 1), 0)
    o_ref[0] = jnp.where(rowi >= PADF, h_ref[0] + out, 0.0)


def _outproj(att, rec, yf, yb, xs, z, h, anw, snw, dsk, wa, wr, ws):
    b, p, _ = h.shape
    tm = _row_tile(p)
    blk = lambda w: pl.BlockSpec((1, tm, w), lambda i, j: (i, j, 0))
    const = lambda a: pl.BlockSpec(a.shape, lambda i, j: (0, 0))
    return pl.pallas_call(
        _outproj_body, grid=(b, p // tm),
        in_specs=[blk(512)] * 6 + [blk(D_MODEL), const(anw), const(snw), const(dsk),
                                   const(wa), const(wr), const(ws)],
        out_specs=blk(D_MODEL), out_shape=jax.ShapeDtypeStruct((b, p, D_MODEL), F32),
        compiler_params=_cparams(("parallel", "parallel")), name="outproj")(
            att, rec, yf, yb, xs, z, h, anw, snw, dsk, wa, wr, ws)


def _router_body(h_ref, nw_ref, wh_ref, wl_ref, xe_ref, afft_ref):
    tp = h_ref.shape[1]
    j = pl.program_id(1)
    h = h_ref[0]
    ms = jnp.mean(h * h, axis=-1, keepdims=True)
    xn = h * lax.rsqrt(ms + EPS) * nw_ref[...]
    x_hi = xn.astype(BF16)
    x_lo = (xn - x_hi.astype(F32)).astype(BF16)
    logits = _dot(x_hi, wh_ref[...]) + _dot(x_hi, wl_ref[...]) + _dot(x_lo, wh_ref[...])
    lane = lax.broadcasted_iota(I32, (tp, LANE), 1)
    real = lane < N_EXPERTS
    m = jnp.max(jnp.where(real, logits, NEG_BIG), axis=-1, keepdims=True)
    pexp = jnp.exp(jnp.where(real, logits - m, 0.0))
    aff = pexp / jnp.sum(jnp.where(real, pexp, 0.0), axis=-1, keepdims=True)
    x_f = x_hi.astype(F32)
    for a in range(D_MODEL // LANE):
        xe_ref[0, pl.ds(a, tp, stride=SUBLANE), :] = x_f[:, a * LANE:(a + 1) * LANE]
    rowi = j * tp + lax.broadcasted_iota(I32, (tp, LANE), 0)
    afft_ref[0] = jnp.where(rowi >= PADF, aff, -1.0).T[:N_EXPERTS, :]


def _router(h, nw, wh, wl):
    b, p, _ = h.shape
    tp = _row_tile(p)
    const = lambda a: pl.BlockSpec(a.shape, lambda i, j: (0, 0))
    return pl.pallas_call(
        _router_body, grid=(b, p // tp),
        in_specs=[pl.BlockSpec((1, tp, D_MODEL), lambda i, j: (i, j, 0)), const(nw), const(wh), const(wl)],
        out_specs=[pl.BlockSpec((1, tp * SUBLANE, LANE), lambda i, j: (i, j, 0)),
                   pl.BlockSpec((1, N_EXPERTS, tp), lambda i, j: (i, 0, j))],
        out_shape=[jax.ShapeDtypeStruct((b, p * SUBLANE, LANE), F32),
                   jax.ShapeDtypeStruct((b, N_EXPERTS, p), F32)],
        compiler_params=_cparams(("parallel", "parallel")), name="router")(h, nw, wh, wl)


def _select_body(cap, a2_ref, a3_ref, us_ref, ls_ref, li_ref, idx_ref, gate_ref, thr_ref, need_ref):
    n = LANE
    cps = idx_ref.shape[2]
    a2 = a2_ref[0]
    bits = lax.bitcast_convert_type(a2, I32)

    def search(i, t):
        cand = t | jnp.left_shift(jnp.int32(1), 30 - i)
        cnt = jnp.sum(jnp.where(bits >= cand, 1.0, 0.0), axis=1, keepdims=True)
        return jnp.where(cnt >= cap, cand, t)

    thr = lax.fori_loop(0, 31, search, jnp.zeros((N_EXPERTS, 1), I32))
    n_gt = jnp.sum(jnp.where(bits > thr, 1.0, 0.0), axis=1, keepdims=True)
    thr_ref[...] = jnp.broadcast_to(lax.bitcast_convert_type(thr, F32), (N_EXPERTS, n))
    need_ref[...] = jnp.broadcast_to(cap - n_gt, (N_EXPERTS, n))

    ones = jnp.ones((n, n), BF16)
    us = us_ref[...]
    ls = ls_ref[...]
    li = li_ref[...]
    c_row = lax.broadcasted_iota(I32, (1, cps), 1).astype(F32)
    c_full = lax.broadcasted_iota(I32, (n, cps), 1).astype(F32)
    k_full = lax.broadcasted_iota(I32, (n, cps), 0).astype(F32)
    reps = cps // n

    def per_expert(e, carry):
        a = a3_ref[0, e]
        vt = jnp.broadcast_to(thr_ref[pl.ds(e, 1), :], (n, n))
        nd = jnp.broadcast_to(need_ref[pl.ds(e, 1), :], (n, n))
        gt = jnp.where(a > vt, 1.0, 0.0)
        eq = jnp.where(a == vt, 1.0, 0.0).astype(BF16)
        eq_before = _dot(eq, us) + _dot(ls, _dot(eq, ones).astype(BF16))
        sel = gt + jnp.where(eq_before < nd, eq.astype(F32), 0.0)
        selb = sel.astype(BF16)
        tile_cnt = _dot(selb, ones)
        off_lo = _dot(ls, tile_cnt.astype(BF16))
        off_hi = off_lo + tile_cnt
        lo = jnp.concatenate([off_lo] * reps, axis=1)
        hi = jnp.concatenate([off_hi] * reps, axis=1)
        in_tile = jnp.where(lo <= c_full, jnp.where(c_full < hi, 1.0, 0.0), 0.0)
        incl = _dot(li, sel.T.astype(BF16))
        cnt_at = _dot(incl.astype(BF16), in_tile.astype(BF16))
        tile_of = jnp.sum(in_tile * k_full, axis=0, keepdims=True)
        rank = c_row - jnp.sum(in_tile * lo, axis=0, keepdims=True)
        local = jnp.sum(jnp.where(cnt_at <= rank, 1.0, 0.0), axis=0, keepdims=True)
        idx = jnp.where(c_row < cap, tile_of * n + local, 0.0)
        idx_ref[0, pl.ds(e, 1), :] = idx.astype(I32)
        a_rows = _dot_const_rhs(a.T, in_tile.astype(BF16), 3)
        gate = jnp.sum(jnp.where(k_full == local, a_rows, 0.0), axis=0, keepdims=True)
        gate_ref[0, pl.ds(e, 1), :] = jnp.where(c_row < cap, gate, 0.0)
        return carry

    lax.fori_loop(0, N_EXPERTS, per_expert, 0)


def _select(afft, a3, us, ls, li, cap, cps):
    b, _, p = afft.shape
    const = lambda a: pl.BlockSpec(a.shape, lambda i: (0, 0))
    return pl.pallas_call(
        functools.partial(_select_body, cap), grid=(b,),
        in_specs=[pl.BlockSpec((1, N_EXPERTS, p), lambda i: (i, 0, 0)),
                  pl.BlockSpec((1, N_EXPERTS, LANE, LANE), lambda i: (i, 0, 0, 0)),
                  const(us), const(ls), const(li)],
        out_specs=[pl.BlockSpec((1, N_EXPERTS, cps), lambda i: (i, 0, 0)),
                   pl.BlockSpec((1, N_EXPERTS, cps), lambda i: (i, 0, 0))],
        out_shape=[jax.ShapeDtypeStruct((b, N_EXPERTS, cps), I32),
                   jax.ShapeDtypeStruct((b, N_EXPERTS, cps), F32)],
        scratch_shapes=[pltpu.VMEM((N_EXPERTS, LANE), F32), pltpu.VMEM((N_EXPERTS, LANE), F32)],
        compiler_params=_cparams(("parallel",)), name="select")(afft, a3, us, ls, li)


def _ffn_body(nf, ntok, idx_ref, gate_ref, xe_ref, wg_ref, wu_ref, wd_ref, y_ref,
              xs_ref, xb_ref, acc_ref, acc3_ref, sem):
    b = pl.program_id(0)
    e = pl.program_id(1)
    f = pl.program_id(2)
    ne = pl.num_programs(1)
    rows = xb_ref.shape[0]
    cps = idx_ref.shape[3] // N_EXPERTS
    chunk = rows // nf
    slot = lax.rem(e, 2)
    tok0 = b * ntok
    tile = lambda r: pl.ds(pl.multiple_of(r * SUBLANE, SUBLANE), SUBLANE)

    def row_copy(c, token, buf):
        return pltpu.make_async_copy(xe_ref.at[tile(tok0 + token), :], xs_ref.at[buf, tile(c), :], sem.at[buf])

    def drain(buf):
        def body(c, carry):
            row_copy(c, 0, buf).wait()
            return carry
        lax.fori_loop(0, rows, body, 0, unroll=8)

    @pl.when((e == 0) & (f == 0))
    def _():
        y_ref[...] = jnp.zeros_like(y_ref)

        def issue(c, carry):
            row_copy(c, idx_ref[0, 0, 0, c], 0).start()
            return carry

        lax.fori_loop(0, rows, issue, 0, unroll=8)

    @pl.when(f == 0)
    def _():
        drain(slot)
        for a in range(D_MODEL // LANE):
            xb_ref[:, a * LANE:(a + 1) * LANE] = xs_ref[slot, pl.ds(a, rows, stride=SUBLANE), :].astype(BF16)

    nxt0 = lax.rem(e + 1, ne) * cps + f * chunk
    for u in range(chunk):
        row_copy(f * chunk + u, idx_ref[0, 0, 0, nxt0 + u], 1 - slot).start()

    x = xb_ref[...]
    a = _dot(x, wg_ref[0, 0])
    u = _dot(x, wu_ref[0, 0])
    part = _dot((_silu(a) * u).astype(BF16), wd_ref[0, 0])

    @pl.when(f == 0)
    def _():
        acc_ref[...] = part

    @pl.when(f > 0)
    def _():
        acc_ref[...] += part

    @pl.when(f == nf - 1)
    def _():
        for a in range(D_MODEL // LANE):
            acc3_ref[pl.ds(a, rows, stride=SUBLANE), :] = acc_ref[:, a * LANE:(a + 1) * LANE]

        def scatter(grp, carry):
            base = grp * SCATTER_GROUP
            off = e * cps + base
            ts = [idx_ref[0, 0, 0, off + u] for u in range(SCATTER_GROUP)]
            vals = [y_ref[0, tile(ts[u]), :] + gate_ref[0, 0, 0, off + u] * acc3_ref[tile(base + u), :]
                    for u in range(SCATTER_GROUP)]
            for u in range(SCATTER_GROUP):
                y_ref[0, tile(ts[u]), :] = vals[u]
            return carry

        lax.fori_loop(0, rows // SCATTER_GROUP, scatter, 0)

    @pl.when((e == ne - 1) & (f == nf - 1))
    def _():
        drain(1 - slot)


def _ffn(idx, gate, xe, wg, wu, wd, layer, rows):
    b, p8, _ = xe.shape
    flat = lambda a: a.reshape(b, 1, 1, -1)
    tf = 512
    nf = EXPERT_FF // tf
    assert rows % nf == 0 and rows % SCATTER_GROUP == 0 and N_EXPERTS % 2 == 0
    smem = pl.BlockSpec((1, 1, 1, idx.shape[1] * idx.shape[2]), lambda i, e, f: (i, 0, 0, 0),
                        memory_space=pltpu.SMEM)
    return pl.pallas_call(
        functools.partial(_ffn_body, nf, p8 // SUBLANE), grid=(b, N_EXPERTS, nf),
        in_specs=[smem, smem, pl.BlockSpec(memory_space=pl.ANY),
                  pl.BlockSpec((1, 1, D_MODEL, tf), lambda i, e, f: (layer, e, 0, f)),
                  pl.BlockSpec((1, 1, D_MODEL, tf), lambda i, e, f: (layer, e, 0, f)),
                  pl.BlockSpec((1, 1, tf, D_MODEL), lambda i, e, f: (layer, e, f, 0))],
        out_specs=pl.BlockSpec((1, p8, LANE), lambda i, e, f: (i, 0, 0)),
        out_shape=jax.ShapeDtypeStruct((b, p8, LANE), F32),
        scratch_shapes=[pltpu.VMEM((2, rows * SUBLANE, LANE), F32), pltpu.VMEM((rows, D_MODEL), BF16),
                        pltpu.VMEM((rows, D_MODEL), F32), pltpu.VMEM((rows * SUBLANE, LANE), F32),
                        pltpu.SemaphoreType.DMA((2,))],
        compiler_params=_cparams(("arbitrary", "arbitrary", "arbitrary")), name="expert_ffn")(
            flat(idx), flat(gate), xe.reshape(b * p8, LANE), wg, wu, wd).reshape(b, p8 // SUBLANE, D_MODEL)


def _final_body(h_ref, y_ref, o_ref):
    o_ref[...] = h_ref[...] + y_ref[...]


def _final_add(h, y, seq):
    b, p, _ = h.shape
    lead_blocks = LEAD // LANE
    src = pl.BlockSpec((1, LANE, D_MODEL), lambda i, j: (i, j + lead_blocks, 0))
    return pl.pallas_call(
        _final_body, grid=(b, seq // LANE), in_specs=[src, src],
        out_specs=pl.BlockSpec((1, LANE, D_MODEL), lambda i, j: (i, j, 0)),
        out_shape=jax.ShapeDtypeStruct((b, seq, D_MODEL), F32),
        compiler_params=_cparams(("parallel", "parallel")), name="final_add")(h, y)


def _rope_tables(seq):
    rows = seq // GRID_W
    row = jnp.repeat(jnp.arange(rows), GRID_W).astype(F32)
    col = (jnp.arange(rows * GRID_W) % GRID_W).astype(F32)
    n_pair = HEAD_DIM // 4
    inv = ROPE_THETA ** (-jnp.arange(n_pair, dtype=F32) / n_pair)
    ang = jnp.concatenate([row[:, None] * inv, col[:, None] * inv], axis=-1)
    ang = jnp.concatenate([jnp.zeros((LEAD, HEAD_DIM // 2), F32), ang], axis=0)
    cos, sin = jnp.cos(ang), jnp.sin(ang)
    cos64 = jnp.concatenate([cos, cos], axis=-1)
    sin64 = jnp.concatenate([-sin, sin], axis=-1)
    scale = HEAD_DIM ** -0.5
    return (jnp.tile(cos64, (1, ATT_HEADS)) * scale, jnp.tile(sin64, (1, ATT_HEADS)) * scale,
            jnp.tile(cos64, (1, ATT_KV_HEADS)), jnp.tile(sin64, (1, ATT_KV_HEADS)))


def _block_mean_matrix(width):
    i = jnp.arange(width)
    return jnp.where((i[:, None] // HEAD_DIM) == (i[None, :] // HEAD_DIM), 1.0 / HEAD_DIM, 0.0).astype(BF16)


def kernel(x, meta_tokens, norm1_w, w_in, q_norm_w, k_norm_w, attn_norm_w, hgrn_lb, hgrn_norm_w, conv_w, conv_b, dt_bias, a_log, d_skip, ssm_norm_w, w_out, norm2_w, router_w, w_gate, w_up, w_down):
    bsz, seq, _ = x.shape
    depth = norm1_w.shape[0]
    assert seq % LANE == 0 and seq % GRID_W == 0 and D_MODEL == SUBLANE * LANE
    p = LEAD + seq
    assert p // LANE <= LANE
    n_tok = N_META + seq
    cap = CAPACITY_FACTOR * n_tok // N_EXPERTS
    rows_ffn = -(-cap // 16) * 16
    cps = -(-cap // LANE) * LANE

    h = jnp.concatenate([jnp.zeros((bsz, PADF, D_MODEL), x.dtype),
                         jnp.broadcast_to(meta_tokens[None].astype(x.dtype), (bsz, N_META, D_MODEL)), x], axis=1)

    tabs = _rope_tables(seq)
    tok = jnp.arange(p)
    att_bias = jnp.where(tok[:LEAD] >= PADF, 0.0, NEG_BIG).astype(F32)[None, :]
    gq = _block_mean_matrix(ATT_WIDTH)
    gk = _block_mean_matrix(KV_WIDTH)
    ii = jnp.arange(LANE)
    tri_lo = (ii[None, :] <= ii[:, None]).astype(BF16)
    tri_up = (ii[None, :] >= ii[:, None]).astype(BF16)
    strict_up = (ii[:, None] < ii[None, :]).astype(BF16)
    strict_lo = (ii[None, :] < ii[:, None]).astype(BF16)

    xor = ii[:, None] ^ ii[None, :]
    msb = jnp.zeros_like(xor)
    for bit in range(7):
        msb = jnp.where(xor >= (1 << bit), 1 << bit, msb)
    lvl_f = jnp.where(ii[:, None] >= ii[None, :], msb, -1).astype(I32)
    lvl_b = jnp.where(ii[:, None] <= ii[None, :], msb, -1).astype(I32)

    soft = jax.nn.softmax(hgrn_lb.astype(F32), axis=1)
    lower_bounds = jnp.clip(jnp.cumsum(soft, axis=1) - soft[:, :1], 0.0, 1.0)

    wg_all, wu_all, wd_all = w_gate.astype(BF16), w_up.astype(BF16), w_down.astype(BF16)
    y = None
    for l in range(depth):
        w = w_in[l]
        o = 0
        pieces = []
        for width in (ATT_WIDTH + 2 * KV_WIDTH, 5 * HGRN_WIDTH, SSM_WIDTH, SSM_CONV_DIM, 2 * SSM_HEADS):
            pieces.append(w[:, o:o + width])
            o += width
        pieces[4] = jnp.pad(pieces[4], ((0, 0), (0, LANE - 2 * SSM_HEADS)))
        ws = [a.astype(BF16) for a in pieces]
        h2, (att_p, hg_p, z_p, xbc_p, dt_p) = _inproj(
            h.reshape(bsz * p, D_MODEL), None if y is None else y.reshape(bsz * p, D_MODEL),
            norm1_w[l][None, :], ws)
        h = h2.reshape(bsz, p, D_MODEL)
        r3 = lambda a: a.reshape(bsz, p, a.shape[-1])
        att_p, hg_p, z_p, xbc_p, dt_p = r3(att_p), r3(hg_p), r3(z_p), r3(xbc_p), r3(dt_p)

        qr, kt, vv = _attn_prep(att_p, tabs, jnp.tile(q_norm_w[l], ATT_HEADS)[None, :],
                                jnp.tile(k_norm_w[l], ATT_KV_HEADS)[None, :], gq, gk)
        att = _attention(qr, kt, vv, att_bias)

        lb_f = lower_bounds[0, l].reshape(HGRN_HEADS, 1, HGRN_DK)
        lb_b = lower_bounds[1, l].reshape(HGRN_HEADS, 1, HGRN_DK)
        o_f = _gla(hg_p, lb_f, tri_lo, lvl_f, False)
        rec = _gla(hg_p, lb_b, tri_up, lvl_b, True, o_f, hgrn_norm_w[l].reshape(HGRN_HEADS, 1, HGRN_DK))

        cw = jnp.pad(conv_w[l].T, ((0, SUBLANE - SSM_CONV), (0, 0)))
        xs, bc = _ssd_conv(xbc_p, cw, conv_b[l][None, :])
        dtb = jnp.pad(dt_bias[l].reshape(-1), (0, LANE - 2 * SSM_HEADS))[None, :]
        arow = jnp.pad(-jnp.exp(a_log[l].astype(F32)).reshape(-1), (0, LANE - 2 * SSM_HEADS))[None, :]
        y_f = _ssd(xs, bc, dt_p, dtb, arow, tri_lo, tri_up, False)
        y_b = _ssd(xs, bc, dt_p, dtb, arow, tri_up, tri_lo, True)

        wo = w_out[l].astype(BF16)
        h = _outproj(att, rec, y_f, y_b, xs, z_p, h, attn_norm_w[l][None, :], ssm_norm_w[l][None, :],
                     jnp.repeat(d_skip[l], SSM_HEAD_DIM)[None, :],
                     wo[:ATT_WIDTH], wo[ATT_WIDTH:ATT_WIDTH + HGRN_WIDTH], wo[ATT_WIDTH + HGRN_WIDTH:])

        wr = jnp.pad(router_w[l], ((0, 0), (0, LANE - N_EXPERTS)))
        wr_hi = wr.astype(BF16)
        wr_lo = (wr - wr_hi.astype(F32)).astype(BF16)
        xe, afft = _router(h, norm2_w[l][None, :], wr_hi, wr_lo)
        nb = p // LANE
        a3 = jnp.pad(afft.reshape(bsz, N_EXPERTS, nb, LANE), ((0, 0), (0, 0), (0, LANE - nb), (0, 0)),
                     constant_values=-1.0)
        idx, gate = _select(afft, a3, strict_up, strict_lo, tri_lo, cap, cps)
        y = _ffn(idx, gate, xe, wg_all, wu_all, wd_all, l, rows_ffn)

    return _final_add(h, y, seq)
```

```python
import functools

import jax
import jax.numpy as jnp
from jax import lax
from jax.experimental import pallas as pl
from jax.experimental.pallas import tpu as pltpu

F32 = jnp.float32
BF16 = jnp.bfloat16
I32 = jnp.int32

D_MODEL = 1024
N_META = 16
GRID_W = 64
ROPE_THETA = 10000.0
EPS = 1e-6
EXP_CLIP = 30.0
LOG2E = 1.4426950408889634

ATT_HEADS = 8
ATT_KV_HEADS = 2
HEAD_DIM = 64
ATT_WIDTH = ATT_HEADS * HEAD_DIM
KV_WIDTH = ATT_KV_HEADS * HEAD_DIM
ATT_REP = ATT_HEADS // ATT_KV_HEADS

HGRN_HEADS = 4
HGRN_DK = 128
HGRN_WIDTH = 512

SSM_HEADS = 8
SSM_HEAD_DIM = 64
SSM_GROUPS = 2
SSM_STATE = 64
SSM_CONV = 7
SSM_WIDTH = 512
SSM_CONV_DIM = 768
SSM_REP = SSM_HEADS // SSM_GROUPS

N_EXPERTS = 16
EXPERT_FF = 2048
CAPACITY_FACTOR = 2

LANE = 128
SUBLANE = 8
LEAD = 128
PADF = LEAD - N_META
NEG_BIG = -1e30
VMEM_LIMIT = 60 * 1024 * 1024
SCATTER_GROUP = 16
GLA_HEADS_PER_STEP = 4


def _cparams(sem):
    return pltpu.CompilerParams(dimension_semantics=sem, vmem_limit_bytes=VMEM_LIMIT)


def _dot(a, b):
    return jnp.dot(a, b, preferred_element_type=F32)


def _dot_nt(a, b):
    return lax.dot_general(a, b, (((1,), (1,)), ((), ())), preferred_element_type=F32)


def _split_bf16(x, n):
    parts, r = [], x
    for _ in range(n):
        p = r.astype(BF16)
        parts.append(p)
        r = r - p.astype(F32)
    return parts


def _dot_const_lhs(c, x, n=3):
    out = None
    for p in _split_bf16(x, n):
        t = _dot(c, p)
        out = t if out is None else out + t
    return out


def _dot_const_rhs(x, c, n=3):
    out = None
    for p in _split_bf16(x, n):
        t = _dot(p, c)
        out = t if out is None else out + t
    return out


def _silu(x):
    return x * jax.nn.sigmoid(x)


def _row_tile(p):
    return 384 if p % 384 == 0 else LANE


def _untile_rows(y_ref, rows):
    return jnp.concatenate([y_ref[pl.ds(a, rows, stride=SUBLANE), :] for a in range(D_MODEL // LANE)], axis=-1)


def _inproj_body(has_y, *refs):
    if has_y:
        h_ref, y_ref, nw_ref = refs[:3]
        ws = refs[3:8]
        hn_ref = refs[8]
        outs = refs[9:14]
        h = h_ref[...] + _untile_rows(y_ref, h_ref.shape[0])
        hn_ref[...] = h
    else:
        h_ref, nw_ref = refs[:2]
        ws = refs[2:7]
        outs = refs[7:12]
        h = h_ref[...]
    ms = jnp.mean(h * h, axis=-1, keepdims=True)
    u = (h * lax.rsqrt(ms + EPS) * nw_ref[...]).astype(BF16)
    for w_ref, o_ref in zip(ws, outs):
        o_ref[...] = _dot(u, w_ref[...])


def _inproj(h2, y2, nw, ws):
    m = h2.shape[0]
    tm = 256 if m % 256 == 0 else LANE
    has_y = y2 is not None
    row = pl.BlockSpec((tm, D_MODEL), lambda i: (i, 0))
    const = lambda a: pl.BlockSpec(a.shape, lambda i: (0, 0))
    yrow = pl.BlockSpec((tm * SUBLANE, LANE), lambda i: (i, 0))
    in_specs = [row] + ([yrow] if has_y else []) + [const(nw)] + [const(w) for w in ws]
    out_shape = ([jax.ShapeDtypeStruct((m, D_MODEL), F32)] if has_y else []) + [
        jax.ShapeDtypeStruct((m, w.shape[1]), F32) for w in ws]
    out_specs = ([row] if has_y else []) + [pl.BlockSpec((tm, w.shape[1]), lambda i: (i, 0)) for w in ws]
    args = [h2] + ([y2] if has_y else []) + [nw] + list(ws)
    res = pl.pallas_call(
        functools.partial(_inproj_body, has_y),
        grid=(m // tm,), in_specs=in_specs, out_specs=out_specs, out_shape=out_shape,
        compiler_params=_cparams(("parallel",)), name="inproj")(*args)
    if has_y:
        return res[0], res[1:]
    return h2, res


def _attn_prep_body(x_ref, cq_ref, sq_ref, ck_ref, sk_ref, qw_ref, kw_ref, gq_ref, gk_ref,
                    q_ref, kt_ref, v_ref):
    x = x_ref[0]
    q = x[:, :ATT_WIDTH]
    k = x[:, ATT_WIDTH:ATT_WIDTH + KV_WIDTH]
    v = x[:, ATT_WIDTH + KV_WIDTH:]

    def head_norm(a, g_ref, w_ref):
        ms = _dot_const_rhs(a * a, g_ref[...], 2)
        return a * lax.rsqrt(ms + EPS) * w_ref[...]

    def rope(a, c_ref, s_ref):
        width = a.shape[1]
        lane = lax.broadcasted_iota(I32, a.shape, 1)
        first = (lane & (HEAD_DIM - 1)) < HEAD_DIM // 2
        partner = jnp.where(first, pltpu.roll(a, width - HEAD_DIM // 2, 1),
                            pltpu.roll(a, HEAD_DIM // 2, 1))
        return a * c_ref[...] + partner * s_ref[...]

    q_ref[0] = rope(head_norm(q, gq_ref, qw_ref), cq_ref, sq_ref).astype(BF16)
    kr = rope(head_norm(k, gk_ref, kw_ref), ck_ref, sk_ref)
    kt_ref[0] = kr.T.astype(BF16)
    ones = jnp.ones((v.shape[0], HEAD_DIM), F32)
    for g in range(ATT_KV_HEADS):
        v_ref[0, g] = jnp.concatenate([v[:, g * HEAD_DIM:(g + 1) * HEAD_DIM], ones], axis=-1).astype(BF16)


def _attn_prep(att_p, tabs, qw, kw, gq, gk):
    b, p, _ = att_p.shape
    tp = _row_tile(p)
    cq, sq, ck, sk = tabs
    tab = lambda a: pl.BlockSpec((tp, a.shape[1]), lambda i, j: (j, 0))
    const = lambda a: pl.BlockSpec(a.shape, lambda i, j: (0, 0))
    return pl.pallas_call(
        _attn_prep_body, grid=(b, p // tp),
        in_specs=[pl.BlockSpec((1, tp, ATT_WIDTH + 2 * KV_WIDTH), lambda i, j: (i, j, 0)),
                  tab(cq), tab(sq), tab(ck), tab(sk), const(qw), const(kw), const(gq), const(gk)],
        out_specs=[pl.BlockSpec((1, tp, ATT_WIDTH), lambda i, j: (i, j, 0)),
                   pl.BlockSpec((1, KV_WIDTH, tp), lambda i, j: (i, 0, j)),
                   pl.BlockSpec((1, ATT_KV_HEADS, tp, 2 * HEAD_DIM), lambda i, j: (i, 0, j, 0))],
        out_shape=[jax.ShapeDtypeStruct((b, p, ATT_WIDTH), BF16),
                   jax.ShapeDtypeStruct((b, KV_WIDTH, p), BF16),
                   jax.ShapeDtypeStruct((b, ATT_KV_HEADS, p, 2 * HEAD_DIM), BF16)],
        compiler_params=_cparams(("parallel", "parallel")), name="attn_prep")(
            att_p, cq, sq, ck, sk, qw, kw, gq, gk)


def _attn_body(q_ref, kt_ref, v_ref, bias_ref, o_ref):
    bias = bias_ref[...]
    heads = [(g, r) for g in range(ATT_KV_HEADS) for r in range(ATT_REP)]

    def scores(g, r):
        h = g * ATT_REP + r
        qh = q_ref[0, :, h * HEAD_DIM:(h + 1) * HEAD_DIM]
        rows = slice(g * HEAD_DIM, (g + 1) * HEAD_DIM)
        return _dot(qh, kt_ref[0, rows, :LEAD]) + bias, _dot(qh, kt_ref[0, rows, LEAD:])

    outs = []
    nxt = scores(*heads[0])
    for i, (g, r) in enumerate(heads):
        s0, s1 = nxt
        if i + 1 < len(heads):
            nxt = scores(*heads[i + 1])
        m = jnp.maximum(jnp.max(s0, axis=-1, keepdims=True), jnp.max(s1, axis=-1, keepdims=True))
        p0 = jnp.exp(s0 - m).astype(BF16)
        p1 = jnp.exp(s1 - m).astype(BF16)
        ov = _dot(p0, v_ref[0, g, :LEAD, :]) + _dot(p1, v_ref[0, g, LEAD:, :])
        outs.append(ov[:, :HEAD_DIM] / ov[:, HEAD_DIM:HEAD_DIM + 1])
    o_ref[0] = jnp.concatenate(outs, axis=-1)


def _attention(q, kt, v, bias):
    b, p, _ = q.shape
    tq = _row_tile(p)
    return pl.pallas_call(
        _attn_body, grid=(b, p // tq),
        in_specs=[pl.BlockSpec((1, tq, ATT_WIDTH), lambda i, j: (i, j, 0)),
                  pl.BlockSpec((1, KV_WIDTH, p), lambda i, j: (i, 0, 0)),
                  pl.BlockSpec((1, ATT_KV_HEADS, p, 2 * HEAD_DIM), lambda i, j: (i, 0, 0, 0)),
                  pl.BlockSpec((1, LEAD), lambda i, j: (0, 0))],
        out_specs=pl.BlockSpec((1, tq, ATT_WIDTH), lambda i, j: (i, j, 0)),
        out_shape=jax.ShapeDtypeStruct((b, p, ATT_WIDTH), F32),
        compiler_params=_cparams(("parallel", "parallel")), name="attention")(q, kt, v, bias)


def _gla_body(backward, finalize, *refs):
    if finalize:
        (q_ref, f_ref, v_ref, lb_ref, tri_ref, lvl_ref, gate_ref, of_ref, nw_ref,
         o_ref, st_ref, cum_ref) = refs
    else:
        q_ref, f_ref, v_ref, lb_ref, tri_ref, lvl_ref, o_ref, st_ref, cum_ref = refs
    hgrp = pl.program_id(1)
    c = pl.program_id(2)
    cc = pl.num_programs(2) - 1 - c if backward else c

    @pl.when(c == 0)
    def _():
        st_ref[...] = jnp.zeros_like(st_ref)

    n = LANE
    row = lax.broadcasted_iota(I32, (n, n), 0)
    valid = (cc * n + row) >= PADF
    r4 = row & 3
    odd = (row & 1) != 0
    lvl = lvl_ref[...]
    tri = tri_ref[...]

    for hd in range(GLA_HEADS_PER_STEP):
        sl = slice(hd * n, (hd + 1) * n)
        x = f_ref[0, :, sl]
        lb = lb_ref[hgrp * GLA_HEADS_PER_STEP + hd]
        g = jnp.minimum(x, 0.0) - jnp.log1p(jnp.exp(-jnp.abs(x)))
        g = g + jnp.log1p(lb * jnp.exp(jnp.minimum(-x, EXP_CLIP)))
        g = jnp.where(valid, jnp.minimum(g, 0.0), 0.0)
        k = jnp.where(valid, 1.0 - jnp.exp(g), 0.0)
        g = g * LOG2E
        q = _silu(q_ref[0, :, sl])
        v = v_ref[0, :, sl]

        cum = _dot_const_lhs(tri, g, 3)
        cum_ref[hd] = cum

        q_bf = q.astype(BF16)
        k_bf = k.astype(BF16)
        att = jnp.where(lvl == 0, _dot_nt(q_bf, k_bf), 0.0)
        g_up = pltpu.roll(g, n - 1, 0)
        g_dn = pltpu.roll(g, 1, 0)
        s = 1
        while s < n:
            if s == 1:
                expo = jnp.where(odd, 0.0, g) if backward else jnp.where(odd, g, 0.0)
            elif s == 2:
                if backward:
                    expo = jnp.where(r4 == 0, g + g_up, jnp.where(r4 == 1, g, jnp.where(r4 == 2, 0.0, g_dn)))
                else:
                    expo = jnp.where(r4 == 0, g_up, jnp.where(r4 == 1, 0.0, jnp.where(r4 == 2, g, g + g_dn)))
            else:
                parts = []
                for blk in range(n // (2 * s)):
                    mid = blk * 2 * s + (s if backward else s - 1)
                    parts.append(jnp.broadcast_to(cum_ref[hd, pl.ds(mid, 1), :], (2 * s, n)))
                ref_cum = parts[0] if len(parts) == 1 else jnp.concatenate(parts, axis=0)
                expo = -jnp.abs(cum - ref_cum)
            e = jnp.exp2(expo).astype(BF16)
            a = _dot_nt(q_bf * e, k_bf * e)
            att = jnp.where(lvl == s, a, att)
            s *= 2

        edge = cum_ref[hd, pl.ds(0 if backward else n - 1, 1), :]
        st = st_ref[hd]
        o = _dot(att.astype(BF16), v.astype(BF16)) + _dot_nt((q * jnp.exp2(cum)).astype(BF16), st.astype(BF16))
        khat = k * jnp.exp2(edge - cum)
        st_ref[hd] = st * jnp.exp2(edge) + _dot(v.T.astype(BF16), khat.astype(BF16))

        if finalize:
            o = o + of_ref[0, :, sl]
            ms = jnp.mean(o * o, axis=-1, keepdims=True)
            o = o * lax.rsqrt(ms + EPS) * nw_ref[hgrp * GLA_HEADS_PER_STEP + hd]
            o = o * _silu(gate_ref[0, :, sl])
        o_ref[0, :, sl] = o


def _gla(hg, lb, tri, lvl, backward, o_fwd=None, nw=None):
    b, p, _ = hg.shape
    nc = p // LANE
    hp = GLA_HEADS_PER_STEP
    width = hp * LANE
    groups = HGRN_HEADS // hp
    finalize = o_fwd is not None
    cidx = (lambda c: nc - 1 - c) if backward else (lambda c: c)
    blk = lambda part: pl.BlockSpec((1, LANE, width), lambda i, h, c: (i, cidx(c), part * groups + h))
    per_head = pl.BlockSpec((HGRN_HEADS, 1, LANE), lambda i, h, c: (0, 0, 0))
    const = pl.BlockSpec((LANE, LANE), lambda i, h, c: (0, 0))
    in_specs = [blk(0), blk(2 if backward else 1), blk(3), per_head, const, const]
    args = [hg, hg, hg, lb, tri, lvl]
    if finalize:
        in_specs += [blk(4), blk(0), per_head]
        args += [hg, o_fwd, nw]
    return pl.pallas_call(
        functools.partial(_gla_body, backward, finalize), grid=(b, groups, nc),
        in_specs=in_specs, out_specs=blk(0),
        out_shape=jax.ShapeDtypeStruct((b, p, HGRN_WIDTH), F32),
        scratch_shapes=[pltpu.VMEM((hp, LANE, LANE), F32), pltpu.VMEM((hp, LANE, LANE), F32)],
        compiler_params=_cparams(("parallel", "parallel", "arbitrary")),
        name="gla_bwd" if backward else "gla_fwd")(*args)


def _conv_body(cur_ref, prev_ref, next_ref, cw_ref, cb_ref, xs_ref, bc_ref, ext_ref):
    j = pl.program_id(1)
    tp = cur_ref.shape[1]
    half = SSM_CONV // 2
    ext_ref[0:SUBLANE, :] = jnp.where(j > 0, prev_ref[0], 0.0)
    ext_ref[SUBLANE:SUBLANE + tp, :] = cur_ref[0]
    ext_ref[SUBLANE + tp:2 * SUBLANE + tp, :] = jnp.where(j < pl.num_programs(1) - 1, next_ref[0], 0.0)
    acc = jnp.broadcast_to(cb_ref[...], (tp, SSM_CONV_DIM))
    for w in range(SSM_CONV):
        acc = acc + ext_ref[pl.ds(SUBLANE - half + w, tp), :] * cw_ref[w:w + 1, :]
    y = _silu(acc)
    xs_ref[0] = y[:, :SSM_WIDTH]
    bc_ref[0] = y[:, SSM_WIDTH:]


def _ssd_conv(xbc, cw, cb):
    b, p, _ = xbc.shape
    tp = _row_tile(p)
    r8 = tp // SUBLANE
    last8 = p // SUBLANE - 1
    return pl.pallas_call(
        _conv_body, grid=(b, p // tp),
        in_specs=[pl.BlockSpec((1, tp, SSM_CONV_DIM), lambda i, j: (i, j, 0)),
                  pl.BlockSpec((1, SUBLANE, SSM_CONV_DIM), lambda i, j: (i, jnp.maximum(j * r8 - 1, 0), 0)),
                  pl.BlockSpec((1, SUBLANE, SSM_CONV_DIM), lambda i, j: (i, jnp.minimum((j + 1) * r8, last8), 0)),
                  pl.BlockSpec(cw.shape, lambda i, j: (0, 0)),
                  pl.BlockSpec(cb.shape, lambda i, j: (0, 0))],
        out_specs=[pl.BlockSpec((1, tp, SSM_WIDTH), lambda i, j: (i, j, 0)),
                   pl.BlockSpec((1, tp, 2 * SSM_GROUPS * SSM_STATE), lambda i, j: (i, j, 0))],
        out_shape=[jax.ShapeDtypeStruct((b, p, SSM_WIDTH), F32),
                   jax.ShapeDtypeStruct((b, p, 2 * SSM_GROUPS * SSM_STATE), F32)],
        scratch_shapes=[pltpu.VMEM((tp + 2 * SUBLANE, SSM_CONV_DIM), F32)],
        compiler_params=_cparams(("parallel", "parallel")), name="ssd_conv")(xbc, xbc, xbc, cw, cb)


def _ssd_body(backward, xs_ref, bc_ref, dt_ref, dtb_ref, a_ref, tri_ref, trit_ref, y_ref, st_ref):
    c = pl.program_id(1)
    cc = pl.num_programs(1) - 1 - c if backward else c

    @pl.when(c == 0)
    def _():
        st_ref[...] = jnp.zeros_like(st_ref)

    n = LANE
    row = lax.broadcasted_iota(I32, (n, n), 0)
    col = lax.broadcasted_iota(I32, (n, n), 1)
    valid = (cc * n + row) >= PADF
    tri = (col >= row) if backward else (col <= row)

    dt = jnp.where(valid, jax.nn.softplus(dt_ref[0] + dtb_ref[...]), 0.0)
    g = dt * a_ref[...]
    cum_col = _dot_const_lhs(tri_ref[...], g, 3)
    g_t = g.T
    dt_t = dt.T
    cum_row = _dot_const_rhs(g_t, trit_ref[...], 3)
    edge_t = 0 if backward else n - 1

    bc = bc_ref[0]
    b_t = bc[:, :SSM_GROUPS * SSM_STATE].T
    cm = bc[:, SSM_GROUPS * SSM_STATE:]
    xs = xs_ref[0]
    outs = []
    for grp in range(SSM_GROUPS):
        c_g = cm[:, grp * SSM_STATE:(grp + 1) * SSM_STATE].astype(BF16)
        bt_g = b_t[grp * SSM_STATE:(grp + 1) * SSM_STATE, :]
        cb = _dot(c_g, bt_g.astype(BF16))
        for hh in range(SSM_REP):
            head = grp * SSM_REP + hh
            ln = (SSM_HEADS if backward else 0) + head
            colv = cum_col[:, ln:ln + 1]
            rowv = cum_row[ln:ln + 1, :]
            dtrow = dt_t[ln:ln + 1, :]
            diff = colv - rowv
            decay = jnp.where(tri, jnp.exp(jnp.where(tri, diff, 0.0)), 0.0)
            att = cb * (decay * dtrow)
            x_h = xs[:, head * SSM_HEAD_DIM:(head + 1) * SSM_HEAD_DIM].astype(BF16)
            st = st_ref[head]
            o = _dot(att.astype(BF16), x_h) + _dot(c_g, st.astype(BF16)) * jnp.exp(colv)
            edge = rowv[:, edge_t:edge_t + 1]
            wrow = dtrow * jnp.exp(edge - rowv)
            st_ref[head] = jnp.exp(edge) * st + _dot((bt_g * wrow).astype(BF16), x_h)
            outs.append(o)
    y_ref[0] = jnp.concatenate(outs, axis=-1)


def _ssd(xs, bc, dtp, dtb, arow, tri, trit, backward):
    b, p, _ = xs.shape
    nc = p // LANE
    cidx = (lambda c: nc - 1 - c) if backward else (lambda c: c)
    blk = lambda w: pl.BlockSpec((1, LANE, w), lambda i, c: (i, cidx(c), 0))
    const = lambda a: pl.BlockSpec(a.shape, lambda i, c: (0, 0))
    return pl.pallas_call(
        functools.partial(_ssd_body, backward), grid=(b, nc),
        in_specs=[blk(SSM_WIDTH), blk(2 * SSM_GROUPS * SSM_STATE), blk(LANE),
                  const(dtb), const(arow), const(tri), const(trit)],
        out_specs=blk(SSM_WIDTH),
        out_shape=jax.ShapeDtypeStruct((b, p, SSM_WIDTH), F32),
        scratch_shapes=[pltpu.VMEM((SSM_HEADS, SSM_STATE, SSM_HEAD_DIM), F32)],
        compiler_params=_cparams(("parallel", "arbitrary")),
        name="ssd_bwd" if backward else "ssd_fwd")(xs, bc, dtp, dtb, arow, tri, trit)


def _outproj_body(att_ref, rec_ref, yf_ref, yb_ref, xs_ref, z_ref, h_ref, anw_ref, snw_ref, dsk_ref,
                  wa_ref, wr_ref, ws_ref, o_ref):
    tm = h_ref.shape[1]
    j = pl.program_id(1)

    def rms(a, w_ref):
        ms = jnp.mean(a * a, axis=-1, keepdims=True)
        return a * lax.rsqrt(ms + EPS) * w_ref[...]

    att = rms(att_ref[0], anw_ref)
    y = yf_ref[0] + yb_ref[0] + dsk_ref[...] * xs_ref[0]
    ssm = rms(y * _silu(z_ref[0]), snw_ref)
    out = (_dot(att.astype(BF16), wa_ref[...]) + _dot(rec_ref[0].astype(BF16), wr_ref[...])
           + _dot(ssm.astype(BF16), ws_ref[...]))
    rowi = j * tm + lax.broadcasted_iota(I32, (tm, 1), 0)
    o_ref[0] = jnp.where(rowi >= PADF, h_ref[0] + out, 0.0)


def _outproj(att, rec, yf, yb, xs, z, h, anw, snw, dsk, wa, wr, ws):
    b, p, _ = h.shape
    tm = _row_tile(p)
    blk = lambda w: pl.BlockSpec((1, tm, w), lambda i, j: (i, j, 0))
    const = lambda a: pl.BlockSpec(a.shape, lambda i, j: (0, 0))
    return pl.pallas_call(
        _outproj_body, grid=(b, p // tm),
        in_specs=[blk(512)] * 6 + [blk(D_MODEL), const(anw), const(snw), const(dsk),
                                   const(wa), const(wr), const(ws)],
        out_specs=blk(D_MODEL), out_shape=jax.ShapeDtypeStruct((b, p, D_MODEL), F32),
        compiler_params=_cparams(("parallel", "parallel")), name="outproj")(
            att, rec, yf, yb, xs, z, h, anw, snw, dsk, wa, wr, ws)


def _router_body(h_ref, nw_ref, wh_ref, wl_ref, xe_ref, afft_ref):
    tp = h_ref.shape[1]
    j = pl.program_id(1)
    h = h_ref[0]
    ms = jnp.mean(h * h, axis=-1, keepdims=True)
    xn = h * lax.rsqrt(ms + EPS) * nw_ref[...]
    x_hi = xn.astype(BF16)
    x_lo = (xn - x_hi.astype(F32)).astype(BF16)
    logits = _dot(x_hi, wh_ref[...]) + _dot(x_hi, wl_ref[...]) + _dot(x_lo, wh_ref[...])
    lane = lax.broadcasted_iota(I32, (tp, LANE), 1)
    real = lane < N_EXPERTS
    m = jnp.max(jnp.where(real, logits, NEG_BIG), axis=-1, keepdims=True)
    pexp = jnp.exp(jnp.where(real, logits - m, 0.0))
    aff = pexp / jnp.sum(jnp.where(real, pexp, 0.0), axis=-1, keepdims=True)
    x_f = x_hi.astype(F32)
    for a in range(D_MODEL // LANE):
        xe_ref[0, pl.ds(a, tp, stride=SUBLANE), :] = x_f[:, a * LANE:(a + 1) * LANE]
    rowi = j * tp + lax.broadcasted_iota(I32, (tp, LANE), 0)
    afft_ref[0] = jnp.where(rowi >= PADF, aff, -1.0).T[:N_EXPERTS, :]


def _router(h, nw, wh, wl):
    b, p, _ = h.shape
    tp = _row_tile(p)
    const = lambda a: pl.BlockSpec(a.shape, lambda i, j: (0, 0))
    return pl.pallas_call(
        _router_body, grid=(b, p // tp),
        in_specs=[pl.BlockSpec((1, tp, D_MODEL), lambda i, j: (i, j, 0)), const(nw), const(wh), const(wl)],
        out_specs=[pl.BlockSpec((1, tp * SUBLANE, LANE), lambda i, j: (i, j, 0)),
                   pl.BlockSpec((1, N_EXPERTS, tp), lambda i, j: (i, 0, j))],
        out_shape=[jax.ShapeDtypeStruct((b, p * SUBLANE, LANE), F32),
                   jax.ShapeDtypeStruct((b, N_EXPERTS, p), F32)],
        compiler_params=_cparams(("parallel", "parallel")), name="router")(h, nw, wh, wl)


def _select_body(cap, a2_ref, a3_ref, us_ref, ls_ref, li_ref, idx_ref, gate_ref, thr_ref, need_ref):
    n = LANE
    cps = idx_ref.shape[2]
    a2 = a2_ref[0]
    bits = lax.bitcast_convert_type(a2, I32)

    def search(i, t):
        cand = t | jnp.left_shift(jnp.int32(1), 30 - i)
        cnt = jnp.sum(jnp.where(bits >= cand, 1.0, 0.0), axis=1, keepdims=True)
        return jnp.where(cnt >= cap, cand, t)

    thr = lax.fori_loop(0, 31, search, jnp.zeros((N_EXPERTS, 1), I32))
    n_gt = jnp.sum(jnp.where(bits > thr, 1.0, 0.0), axis=1, keepdims=True)
    thr_ref[...] = jnp.broadcast_to(lax.bitcast_convert_type(thr, F32), (N_EXPERTS, n))
    need_ref[...] = jnp.broadcast_to(cap - n_gt, (N_EXPERTS, n))

    ones = jnp.ones((n, n), BF16)
    us = us_ref[...]
    ls = ls_ref[...]
    li = li_ref[...]
    c_row = lax.broadcasted_iota(I32, (1, cps), 1).astype(F32)
    c_full = lax.broadcasted_iota(I32, (n, cps), 1).astype(F32)
    k_full = lax.broadcasted_iota(I32, (n, cps), 0).astype(F32)
    reps = cps // n

    def per_expert(e, carry):
        a = a3_ref[0, e]
        vt = jnp.broadcast_to(thr_ref[pl.ds(e, 1), :], (n, n))
        nd = jnp.broadcast_to(need_ref[pl.ds(e, 1), :], (n, n))
        gt = jnp.where(a > vt, 1.0, 0.0)
        eq = jnp.where(a == vt, 1.0, 0.0).astype(BF16)
        eq_before = _dot(eq, us) + _dot(ls, _dot(eq, ones).astype(BF16))
        sel = gt + jnp.where(eq_before < nd, eq.astype(F32), 0.0)
        selb = sel.astype(BF16)
        tile_cnt = _dot(selb, ones)
        off_lo = _dot(ls, tile_cnt.astype(BF16))
        off_hi = off_lo + tile_cnt
        lo = jnp.concatenate([off_lo] * reps, axis=1)
        hi = jnp.concatenate([off_hi] * reps, axis=1)
        in_tile = jnp.where(lo <= c_full, jnp.where(c_full < hi, 1.0, 0.0), 0.0)
        incl = _dot(li, sel.T.astype(BF16))
        cnt_at = _dot(incl.astype(BF16), in_tile.astype(BF16))
        tile_of = jnp.sum(in_tile * k_full, axis=0, keepdims=True)
        rank = c_row - jnp.sum(in_tile * lo, axis=0, keepdims=True)
        local = jnp.sum(jnp.where(cnt_at <= rank, 1.0, 0.0), axis=0, keepdims=True)
        idx = jnp.where(c_row < cap, tile_of * n + local, 0.0)
        idx_ref[0, pl.ds(e, 1), :] = idx.astype(I32)
        a_rows = _dot_const_rhs(a.T, in_tile.astype(BF16), 3)
        gate = jnp.sum(jnp.where(k_full == local, a_rows, 0.0), axis=0, keepdims=True)
        gate_ref[0, pl.ds(e, 1), :] = jnp.where(c_row < cap, gate, 0.0)
        return carry

    lax.fori_loop(0, N_EXPERTS, per_expert, 0)


def _select(afft, a3, us, ls, li, cap, cps):
    b, _, p = afft.shape
    const = lambda a: pl.BlockSpec(a.shape, lambda i: (0, 0))
    return pl.pallas_call(
        functools.partial(_select_body, cap), grid=(b,),
        in_specs=[pl.BlockSpec((1, N_EXPERTS, p), lambda i: (i, 0, 0)),
                  pl.BlockSpec((1, N_EXPERTS, LANE, LANE), lambda i: (i, 0, 0, 0)),
                  const(us), const(ls), const(li)],
        out_specs=[pl.BlockSpec((1, N_EXPERTS, cps), lambda i: (i, 0, 0)),
                   pl.BlockSpec((1, N_EXPERTS, cps), lambda i: (i, 0, 0))],
        out_shape=[jax.ShapeDtypeStruct((b, N_EXPERTS, cps), I32),
                   jax.ShapeDtypeStruct((b, N_EXPERTS, cps), F32)],
        scratch_shapes=[pltpu.VMEM((N_EXPERTS, LANE), F32), pltpu.VMEM((N_EXPERTS, LANE), F32)],
        compiler_params=_cparams(("parallel",)), name="select")(afft, a3, us, ls, li)


def _ffn_body(nf, ntok, idx_ref, gate_ref, xe_ref, wg_ref, wu_ref, wd_ref, y_ref,
              xs_ref, xb_ref, acc_ref, acc3_ref, sem):
    b = pl.program_id(0)
    e = pl.program_id(1)
    f = pl.program_id(2)
    ne = pl.num_programs(1)
    rows = xb_ref.shape[0]
    cps = idx_ref.shape[3] // N_EXPERTS
    chunk = rows // nf
    slot = lax.rem(e, 2)
    tok0 = b * ntok
    tile = lambda r: pl.ds(pl.multiple_of(r * SUBLANE, SUBLANE), SUBLANE)

    def row_copy(c, token, buf):
        return pltpu.make_async_copy(xe_ref.at[tile(tok0 + token), :], xs_ref.at[buf, tile(c), :], sem.at[buf])

    def drain(buf):
        def body(c, carry):
            row_copy(c, 0, buf).wait()
            return carry
        lax.fori_loop(0, rows, body, 0, unroll=8)

    @pl.when((e == 0) & (f == 0))
    def _():
        y_ref[...] = jnp.zeros_like(y_ref)

        def issue(c, carry):
            row_copy(c, idx_ref[0, 0, 0, c], 0).start()
            return carry

        lax.fori_loop(0, rows, issue, 0, unroll=8)

    @pl.when(f == 0)
    def _():
        drain(slot)
        for a in range(D_MODEL // LANE):
            xb_ref[:, a * LANE:(a + 1) * LANE] = xs_ref[slot, pl.ds(a, rows, stride=SUBLANE), :].astype(BF16)

    nxt0 = lax.rem(e + 1, ne) * cps + f * chunk
    for u in range(chunk):
        row_copy(f * chunk + u, idx_ref[0, 0, 0, nxt0 + u], 1 - slot).start()

    x = xb_ref[...]
    a = _dot(x, wg_ref[0, 0])
    u = _dot(x, wu_ref[0, 0])
    part = _dot((_silu(a) * u).astype(BF16), wd_ref[0, 0])

    @pl.when(f == 0)
    def _():
        acc_ref[...] = part

    @pl.when(f > 0)
    def _():
        acc_ref[...] += part

    @pl.when(f == nf - 1)
    def _():
        for a in range(D_MODEL // LANE):
            acc3_ref[pl.ds(a, rows, stride=SUBLANE), :] = acc_ref[:, a * LANE:(a + 1) * LANE]

        def scatter(grp, carry):
            base = grp * SCATTER_GROUP
            off = e * cps + base
            ts = [idx_ref[0, 0, 0, off + u] for u in range(SCATTER_GROUP)]
            vals = [y_ref[0, tile(ts[u]), :] + gate_ref[0, 0, 0, off + u] * acc3_ref[tile(base + u), :]
                    for u in range(SCATTER_GROUP)]
            for u in range(SCATTER_GROUP):
                y_ref[0, tile(ts[u]), :] = vals[u]
            return carry

        lax.fori_loop(0, rows // SCATTER_GROUP, scatter, 0)

    @pl.when((e == ne - 1) & (f == nf - 1))
    def _():
        drain(1 - slot)


def _ffn(idx, gate, xe, wg, wu, wd, layer, rows):
    b, p8, _ = xe.shape
    flat = lambda a: a.reshape(b, 1, 1, -1)
    tf = 2048
    nf = EXPERT_FF // tf
    assert rows % nf == 0 and rows % SCATTER_GROUP == 0 and N_EXPERTS % 2 == 0
    smem = pl.BlockSpec((1, 1, 1, idx.shape[1] * idx.shape[2]), lambda i, e, f: (i, 0, 0, 0),
                        memory_space=pltpu.SMEM)
    return pl.pallas_call(
        functools.partial(_ffn_body, nf, p8 // SUBLANE), grid=(b, N_EXPERTS, nf),
        in_specs=[smem, smem, pl.BlockSpec(memory_space=pl.ANY),
                  pl.BlockSpec((1, 1, D_MODEL, tf), lambda i, e, f: (layer, e, 0, f)),
                  pl.BlockSpec((1, 1, D_MODEL, tf), lambda i, e, f: (layer, e, 0, f)),
                  pl.BlockSpec((1, 1, tf, D_MODEL), lambda i, e, f: (layer, e, f, 0))],
        out_specs=pl.BlockSpec((1, p8, LANE), lambda i, e, f: (i, 0, 0), pipeline_mode=pl.Buffered(1)),
        out_shape=jax.ShapeDtypeStruct((b, p8, LANE), F32),
        scratch_shapes=[pltpu.VMEM((2, rows * SUBLANE, LANE), F32), pltpu.VMEM((rows, D_MODEL), BF16),
                        pltpu.VMEM((rows, D_MODEL), F32), pltpu.VMEM((rows * SUBLANE, LANE), F32),
                        pltpu.SemaphoreType.DMA((2,))],
        compiler_params=_cparams(("arbitrary", "arbitrary", "arbitrary")), name="expert_ffn")(
            flat(idx), flat(gate), xe.reshape(b * p8, LANE), wg, wu, wd)


def _final_body(h_ref, y_ref, o_ref):
    o_ref[0] = h_ref[0] + _untile_rows(y_ref.at[0], LANE)


def _final_add(h, y, seq):
    b, p, _ = h.shape
    lead_blocks = LEAD // LANE
    src = pl.BlockSpec((1, LANE, D_MODEL), lambda i, j: (i, j + lead_blocks, 0))
    return pl.pallas_call(
        _final_body, grid=(b, seq // LANE),
        in_specs=[src, pl.BlockSpec((1, LANE * SUBLANE, LANE), lambda i, j: (i, j + lead_blocks, 0))],
        out_specs=pl.BlockSpec((1, LANE, D_MODEL), lambda i, j: (i, j, 0)),
        out_shape=jax.ShapeDtypeStruct((b, seq, D_MODEL), F32),
        compiler_params=_cparams(("parallel", "parallel")), name="final_add")(h, y)


def _rope_tables(seq):
    rows = seq // GRID_W
    row = jnp.repeat(jnp.arange(rows), GRID_W).astype(F32)
    col = (jnp.arange(rows * GRID_W) % GRID_W).astype(F32)
    n_pair = HEAD_DIM // 4
    inv = ROPE_THETA ** (-jnp.arange(n_pair, dtype=F32) / n_pair)
    ang = jnp.concatenate([row[:, None] * inv, col[:, None] * inv], axis=-1)
    ang = jnp.concatenate([jnp.zeros((LEAD, HEAD_DIM // 2), F32), ang], axis=0)
    cos, sin = jnp.cos(ang), jnp.sin(ang)
    cos64 = jnp.concatenate([cos, cos], axis=-1)
    sin64 = jnp.concatenate([-sin, sin], axis=-1)
    scale = HEAD_DIM ** -0.5
    return (jnp.tile(cos64, (1, ATT_HEADS)) * scale, jnp.tile(sin64, (1, ATT_HEADS)) * scale,
            jnp.tile(cos64, (1, ATT_KV_HEADS)), jnp.tile(sin64, (1, ATT_KV_HEADS)))


def _block_mean_matrix(width):
    i = jnp.arange(width)
    return jnp.where((i[:, None] // HEAD_DIM) == (i[None, :] // HEAD_DIM), 1.0 / HEAD_DIM, 0.0).astype(BF16)


def kernel(x, meta_tokens, norm1_w, w_in, q_norm_w, k_norm_w, attn_norm_w, hgrn_lb, hgrn_norm_w, conv_w, conv_b, dt_bias, a_log, d_skip, ssm_norm_w, w_out, norm2_w, router_w, w_gate, w_up, w_down):
    bsz, seq, _ = x.shape
    depth = norm1_w.shape[0]
    assert seq % LANE == 0 and seq % GRID_W == 0 and D_MODEL == SUBLANE * LANE
    p = LEAD + seq
    assert p // LANE <= LANE
    n_tok = N_META + seq
    cap = CAPACITY_FACTOR * n_tok // N_EXPERTS
    rows_ffn = -(-cap // 16) * 16
    cps = -(-cap // LANE) * LANE

    h = jnp.concatenate([jnp.zeros((bsz, PADF, D_MODEL), x.dtype),
                         jnp.broadcast_to(meta_tokens[None].astype(x.dtype), (bsz, N_META, D_MODEL)), x], axis=1)

    tabs = _rope_tables(seq)
    tok = jnp.arange(p)
    att_bias = jnp.where(tok[:LEAD] >= PADF, 0.0, NEG_BIG).astype(F32)[None, :]
    gq = _block_mean_matrix(ATT_WIDTH)
    gk = _block_mean_matrix(KV_WIDTH)
    ii = jnp.arange(LANE)
    tri_lo = (ii[None, :] <= ii[:, None]).astype(BF16)
    tri_up = (ii[None, :] >= ii[:, None]).astype(BF16)
    strict_up = (ii[:, None] < ii[None, :]).astype(BF16)
    strict_lo = (ii[None, :] < ii[:, None]).astype(BF16)

    xor = ii[:, None] ^ ii[None, :]
    msb = jnp.zeros_like(xor)
    for bit in range(7):
        msb = jnp.where(xor >= (1 << bit), 1 << bit, msb)
    lvl_f = jnp.where(ii[:, None] >= ii[None, :], msb, -1).astype(I32)
    lvl_b = jnp.where(ii[:, None] <= ii[None, :], msb, -1).astype(I32)

    soft = jax.nn.softmax(hgrn_lb.astype(F32), axis=1)
    lower_bounds = jnp.clip(jnp.cumsum(soft, axis=1) - soft[:, :1], 0.0, 1.0)

    wg_all, wu_all, wd_all = w_gate.astype(BF16), w_up.astype(BF16), w_down.astype(BF16)
    y = None
    for l in range(depth):
        w = w_in[l]
        o = 0
        pieces = []
        for width in (ATT_WIDTH + 2 * KV_WIDTH, 5 * HGRN_WIDTH, SSM_WIDTH, SSM_CONV_DIM, 2 * SSM_HEADS):
            pieces.append(w[:, o:o + width])
            o += width
        pieces[4] = jnp.pad(pieces[4], ((0, 0), (0, LANE - 2 * SSM_HEADS)))
        ws = [a.astype(BF16) for a in pieces]
        h2, (att_p, hg_p, z_p, xbc_p, dt_p) = _inproj(
            h.reshape(bsz * p, D_MODEL), None if y is None else y.reshape(bsz * p * SUBLANE, LANE),
            norm1_w[l][None, :], ws)
        h = h2.reshape(bsz, p, D_MODEL)
        r3 = lambda a: a.reshape(bsz, p, a.shape[-1])
        att_p, hg_p, z_p, xbc_p, dt_p = r3(att_p), r3(hg_p), r3(z_p), r3(xbc_p), r3(dt_p)

        qr, kt, vv = _attn_prep(att_p, tabs, jnp.tile(q_norm_w[l], ATT_HEADS)[None, :],
                                jnp.tile(k_norm_w[l], ATT_KV_HEADS)[None, :], gq, gk)
        att = _attention(qr, kt, vv, att_bias)

        lb_f = lower_bounds[0, l].reshape(HGRN_HEADS, 1, HGRN_DK)
        lb_b = lower_bounds[1, l].reshape(HGRN_HEADS, 1, HGRN_DK)
        o_f = _gla(hg_p, lb_f, tri_lo, lvl_f, False)
        rec = _gla(hg_p, lb_b, tri_up, lvl_b, True, o_f, hgrn_norm_w[l].reshape(HGRN_HEADS, 1, HGRN_DK))

        cw = jnp.pad(conv_w[l].T, ((0, SUBLANE - SSM_CONV), (0, 0)))
        xs, bc = _ssd_conv(xbc_p, cw, conv_b[l][None, :])
        dtb = jnp.pad(dt_bias[l].reshape(-1), (0, LANE - 2 * SSM_HEADS))[None, :]
        arow = jnp.pad(-jnp.exp(a_log[l].astype(F32)).reshape(-1), (0, LANE - 2 * SSM_HEADS))[None, :]
        y_f = _ssd(xs, bc, dt_p, dtb, arow, tri_lo, tri_up, False)
        y_b = _ssd(xs, bc, dt_p, dtb, arow, tri_up, tri_lo, True)

        wo = w_out[l].astype(BF16)
        h = _outproj(att, rec, y_f, y_b, xs, z_p, h, attn_norm_w[l][None, :], ssm_norm_w[l][None, :],
                     jnp.repeat(d_skip[l], SSM_HEAD_DIM)[None, :],
                     wo[:ATT_WIDTH], wo[ATT_WIDTH:ATT_WIDTH + HGRN_WIDTH], wo[ATT_WIDTH + HGRN_WIDTH:])

        wr = jnp.pad(router_w[l], ((0, 0), (0, LANE - N_EXPERTS)))
        wr_hi = wr.astype(BF16)
        wr_lo = (wr - wr_hi.astype(F32)).astype(BF16)
        xe, afft = _router(h, norm2_w[l][None, :], wr_hi, wr_lo)
        nb = p // LANE
        a3 = jnp.pad(afft.reshape(bsz, N_EXPERTS, nb, LANE), ((0, 0), (0, 0), (0, LANE - nb), (0, 0)),
                     constant_values=-1.0)
        idx, gate = _select(afft, a3, strict_up, strict_lo, tri_lo, cap, cps)
        y = _ffn(idx, gate, xe, wg_all, wu_all, wd_all, l, rows_ffn)

    return _final_add(h, y, seq)
```

```python
import functools

import jax
import jax.numpy as jnp
from jax import lax
from jax.experimental import pallas as pl
from jax.experimental.pallas import tpu as pltpu

F32 = jnp.float32
BF16 = jnp.bfloat16
I32 = jnp.int32

D_MODEL = 1024
N_META = 16
GRID_W = 64
ROPE_THETA = 10000.0
EPS = 1e-6
EXP_CLIP = 30.0
LOG2E = 1.4426950408889634

ATT_HEADS = 8
ATT_KV_HEADS = 2
HEAD_DIM = 64
ATT_WIDTH = ATT_HEADS * HEAD_DIM
KV_WIDTH = ATT_KV_HEADS * HEAD_DIM
ATT_REP = ATT_HEADS // ATT_KV_HEADS

HGRN_HEADS = 4
HGRN_DK = 128
HGRN_WIDTH = 512

SSM_HEADS = 8
SSM_HEAD_DIM = 64
SSM_GROUPS = 2
SSM_STATE = 64
SSM_CONV = 7
SSM_WIDTH = 512
SSM_CONV_DIM = 768
SSM_REP = SSM_HEADS // SSM_GROUPS

N_EXPERTS = 16
EXPERT_FF = 2048
CAPACITY_FACTOR = 2

LANE = 128
SUBLANE = 8
LEAD = 128
PADF = LEAD - N_META
NEG_BIG = -1e30
VMEM_LIMIT = 60 * 1024 * 1024
SCATTER_GROUP = 16
GLA_HEADS_PER_STEP = 4


def _cparams(sem):
    return pltpu.CompilerParams(dimension_semantics=sem, vmem_limit_bytes=VMEM_LIMIT)


def _dot(a, b):
    return jnp.dot(a, b, preferred_element_type=F32)


def _dot_nt(a, b):
    return lax.dot_general(a, b, (((1,), (1,)), ((), ())), preferred_element_type=F32)


def _split_bf16(x, n):
    parts, r = [], x
    for _ in range(n):
        p = r.astype(BF16)
        parts.append(p)
        r = r - p.astype(F32)
    return parts


def _dot_const_lhs(c, x, n=3):
    out = None
    for p in _split_bf16(x, n):
        t = _dot(c, p)
        out = t if out is None else out + t
    return out


def _dot_const_rhs(x, c, n=3):
    out = None
    for p in _split_bf16(x, n):
        t = _dot(p, c)
        out = t if out is None else out + t
    return out


def _silu(x):
    return x * jax.nn.sigmoid(x)


def _row_tile(p):
    return 384 if p % 384 == 0 else LANE


def _untile_rows(y_ref, rows):
    return jnp.concatenate([y_ref[pl.ds(a, rows, stride=SUBLANE), :] for a in range(D_MODEL // LANE)], axis=-1)


def _inproj_body(has_y, *refs):
    if has_y:
        h_ref, y_ref, nw_ref = refs[:3]
        ws = refs[3:8]
        hn_ref = refs[8]
        outs = refs[9:14]
        h = h_ref[...] + _untile_rows(y_ref, h_ref.shape[0])
        hn_ref[...] = h
    else:
        h_ref, nw_ref = refs[:2]
        ws = refs[2:7]
        outs = refs[7:12]
        h = h_ref[...]
    ms = jnp.mean(h * h, axis=-1, keepdims=True)
    u = (h * lax.rsqrt(ms + EPS) * nw_ref[...]).astype(BF16)
    for w_ref, o_ref in zip(ws, outs):
        o_ref[...] = _dot(u, w_ref[...])


def _inproj(h2, y2, nw, ws):
    m = h2.shape[0]
    tm = 256 if m % 256 == 0 else LANE
    has_y = y2 is not None
    row = pl.BlockSpec((tm, D_MODEL), lambda i: (i, 0))
    const = lambda a: pl.BlockSpec(a.shape, lambda i: (0, 0))
    yrow = pl.BlockSpec((tm * SUBLANE, LANE), lambda i: (i, 0))
    in_specs = [row] + ([yrow] if has_y else []) + [const(nw)] + [const(w) for w in ws]
    out_shape = ([jax.ShapeDtypeStruct((m, D_MODEL), F32)] if has_y else []) + [
        jax.ShapeDtypeStruct((m, w.shape[1]), F32) for w in ws]
    out_specs = ([row] if has_y else []) + [pl.BlockSpec((tm, w.shape[1]), lambda i: (i, 0)) for w in ws]
    args = [h2] + ([y2] if has_y else []) + [nw] + list(ws)
    res = pl.pallas_call(
        functools.partial(_inproj_body, has_y),
        grid=(m // tm,), in_specs=in_specs, out_specs=out_specs, out_shape=out_shape,
        compiler_params=_cparams(("parallel",)), name="inproj")(*args)
    if has_y:
        return res[0], res[1:]
    return h2, res


def _attn_prep_body(x_ref, cq_ref, sq_ref, ck_ref, sk_ref, qw_ref, kw_ref, gq_ref, gk_ref,
                    q_ref, kt_ref, v_ref):
    x = x_ref[0]
    q = x[:, :ATT_WIDTH]
    k = x[:, ATT_WIDTH:ATT_WIDTH + KV_WIDTH]
    v = x[:, ATT_WIDTH + KV_WIDTH:]

    def head_norm(a, g_ref, w_ref):
        ms = _dot_const_rhs(a * a, g_ref[...], 2)
        return a * lax.rsqrt(ms + EPS) * w_ref[...]

    def rope(a, c_ref, s_ref):
        width = a.shape[1]
        lane = lax.broadcasted_iota(I32, a.shape, 1)
        first = (lane & (HEAD_DIM - 1)) < HEAD_DIM // 2
        partner = jnp.where(first, pltpu.roll(a, width - HEAD_DIM // 2, 1),
                            pltpu.roll(a, HEAD_DIM // 2, 1))
        return a * c_ref[...] + partner * s_ref[...]

    q_ref[0] = rope(head_norm(q, gq_ref, qw_ref), cq_ref, sq_ref).astype(BF16)
    kr = rope(head_norm(k, gk_ref, kw_ref), ck_ref, sk_ref)
    kt_ref[0] = kr.T.astype(BF16)
    ones = jnp.ones((v.shape[0], HEAD_DIM), F32)
    for g in range(ATT_KV_HEADS):
        v_ref[0, g] = jnp.concatenate([v[:, g * HEAD_DIM:(g + 1) * HEAD_DIM], ones], axis=-1).astype(BF16)


def _attn_prep(att_p, tabs, qw, kw, gq, gk):
    b, p, _ = att_p.shape
    tp = _row_tile(p)
    cq, sq, ck, sk = tabs
    tab = lambda a: pl.BlockSpec((tp, a.shape[1]), lambda i, j: (j, 0))
    const = lambda a: pl.BlockSpec(a.shape, lambda i, j: (0, 0))
    return pl.pallas_call(
        _attn_prep_body, grid=(b, p // tp),
        in_specs=[pl.BlockSpec((1, tp, ATT_WIDTH + 2 * KV_WIDTH), lambda i, j: (i, j, 0)),
                  tab(cq), tab(sq), tab(ck), tab(sk), const(qw), const(kw), const(gq), const(gk)],
        out_specs=[pl.BlockSpec((1, tp, ATT_WIDTH), lambda i, j: (i, j, 0)),
                   pl.BlockSpec((1, KV_WIDTH, tp), lambda i, j: (i, 0, j)),
                   pl.BlockSpec((1, ATT_KV_HEADS, tp, 2 * HEAD_DIM), lambda i, j: (i, 0, j, 0))],
        out_shape=[jax.ShapeDtypeStruct((b, p, ATT_WIDTH), BF16),
                   jax.ShapeDtypeStruct((b, KV_WIDTH, p), BF16),
                   jax.ShapeDtypeStruct((b, ATT_KV_HEADS, p, 2 * HEAD_DIM), BF16)],
        compiler_params=_cparams(("parallel", "parallel")), name="attn_prep")(
            att_p, cq, sq, ck, sk, qw, kw, gq, gk)


def _attn_body(q_ref, kt_ref, v_ref, bias_ref, o_ref):
    bias = bias_ref[...]
    heads = [(g, r) for g in range(ATT_KV_HEADS) for r in range(ATT_REP)]

    def scores(g, r):
        h = g * ATT_REP + r
        qh = q_ref[0, :, h * HEAD_DIM:(h + 1) * HEAD_DIM]
        rows = slice(g * HEAD_DIM, (g + 1) * HEAD_DIM)
        return _dot(qh, kt_ref[0, rows, :LEAD]) + bias, _dot(qh, kt_ref[0, rows, LEAD:])

    outs = []
    nxt = scores(*heads[0])
    for i, (g, r) in enumerate(heads):
        s0, s1 = nxt
        if i + 1 < len(heads):
            nxt = scores(*heads[i + 1])
        m = jnp.maximum(jnp.max(s0, axis=-1, keepdims=True), jnp.max(s1, axis=-1, keepdims=True))
        p0 = jnp.exp(s0 - m).astype(BF16)
        p1 = jnp.exp(s1 - m).astype(BF16)
        ov = _dot(p0, v_ref[0, g, :LEAD, :]) + _dot(p1, v_ref[0, g, LEAD:, :])
        outs.append(ov[:, :HEAD_DIM] / ov[:, HEAD_DIM:HEAD_DIM + 1])
    o_ref[0] = jnp.concatenate(outs, axis=-1)


def _attention(q, kt, v, bias):
    b, p, _ = q.shape
    tq = _row_tile(p)
    return pl.pallas_call(
        _attn_body, grid=(b, p // tq),
        in_specs=[pl.BlockSpec((1, tq, ATT_WIDTH), lambda i, j: (i, j, 0)),
                  pl.BlockSpec((1, KV_WIDTH, p), lambda i, j: (i, 0, 0)),
                  pl.BlockSpec((1, ATT_KV_HEADS, p, 2 * HEAD_DIM), lambda i, j: (i, 0, 0, 0)),
                  pl.BlockSpec((1, LEAD), lambda i, j: (0, 0))],
        out_specs=pl.BlockSpec((1, tq, ATT_WIDTH), lambda i, j: (i, j, 0)),
        out_shape=jax.ShapeDtypeStruct((b, p, ATT_WIDTH), F32),
        compiler_params=_cparams(("parallel", "parallel")), name="attention")(q, kt, v, bias)


def _gla_body(backward, finalize, *refs):
    if finalize:
        (q_ref, f_ref, v_ref, lb_ref, tri_ref, lvl_ref, gate_ref, of_ref, nw_ref,
         o_ref, st_ref, cum_ref) = refs
    else:
        q_ref, f_ref, v_ref, lb_ref, tri_ref, lvl_ref, o_ref, st_ref, cum_ref = refs
    hgrp = pl.program_id(1)
    c = pl.program_id(2)
    cc = pl.num_programs(2) - 1 - c if backward else c

    @pl.when(c == 0)
    def _():
        st_ref[...] = jnp.zeros_like(st_ref)

    n = LANE
    row = lax.broadcasted_iota(I32, (n, n), 0)
    valid = (cc * n + row) >= PADF
    r4 = row & 3
    odd = (row & 1) != 0
    lvl = lvl_ref[...]
    tri = tri_ref[...]

    for hd in range(GLA_HEADS_PER_STEP):
        sl = slice(hd * n, (hd + 1) * n)
        x = f_ref[0, :, sl]
        lb = lb_ref[hgrp * GLA_HEADS_PER_STEP + hd]
        g = jnp.minimum(x, 0.0) - jnp.log1p(jnp.exp(-jnp.abs(x)))
        g = g + jnp.log1p(lb * jnp.exp(jnp.minimum(-x, EXP_CLIP)))
        g = jnp.where(valid, jnp.minimum(g, 0.0), 0.0)
        k = jnp.where(valid, 1.0 - jnp.exp(g), 0.0)
        g = g * LOG2E
        q = _silu(q_ref[0, :, sl])
        v = v_ref[0, :, sl]

        cum = _dot_const_lhs(tri, g, 3)
        cum_ref[hd] = cum

        q_bf = q.astype(BF16)
        k_bf = k.astype(BF16)
        att = jnp.where(lvl == 0, _dot_nt(q_bf, k_bf), 0.0)
        g_up = pltpu.roll(g, n - 1, 0)
        g_dn = pltpu.roll(g, 1, 0)
        s = 1
        while s < n:
            if s == 1:
                expo = jnp.where(odd, 0.0, g) if backward else jnp.where(odd, g, 0.0)
            elif s == 2:
                if backward:
                    expo = jnp.where(r4 == 0, g + g_up, jnp.where(r4 == 1, g, jnp.where(r4 == 2, 0.0, g_dn)))
                else:
                    expo = jnp.where(r4 == 0, g_up, jnp.where(r4 == 1, 0.0, jnp.where(r4 == 2, g, g + g_dn)))
            else:
                parts = []
                for blk in range(n // (2 * s)):
                    mid = blk * 2 * s + (s if backward else s - 1)
                    parts.append(jnp.broadcast_to(cum_ref[hd, pl.ds(mid, 1), :], (2 * s, n)))
                ref_cum = parts[0] if len(parts) == 1 else jnp.concatenate(parts, axis=0)
                expo = -jnp.abs(cum - ref_cum)
            e = jnp.exp2(expo).astype(BF16)
            a = _dot_nt(q_bf * e, k_bf * e)
            att = jnp.where(lvl == s, a, att)
            s *= 2

        edge = cum_ref[hd, pl.ds(0 if backward else n - 1, 1), :]
        st = st_ref[hd]
        o = _dot(att.astype(BF16), v.astype(BF16)) + _dot_nt((q * jnp.exp2(cum)).astype(BF16), st.astype(BF16))
        khat = k * jnp.exp2(edge - cum)
        st_ref[hd] = st * jnp.exp2(edge) + _dot(v.T.astype(BF16), khat.astype(BF16))

        if finalize:
            o = o + of_ref[0, :, sl]
            ms = jnp.mean(o * o, axis=-1, keepdims=True)
            o = o * lax.rsqrt(ms + EPS) * nw_ref[hgrp * GLA_HEADS_PER_STEP + hd]
            o = o * _silu(gate_ref[0, :, sl])
        o_ref[0, :, sl] = o


def _gla(hg, lb, tri, lvl, backward, o_fwd=None, nw=None):
    b, p, _ = hg.shape
    nc = p // LANE
    hp = GLA_HEADS_PER_STEP
    width = hp * LANE
    groups = HGRN_HEADS // hp
    finalize = o_fwd is not None
    cidx = (lambda c: nc - 1 - c) if backward else (lambda c: c)
    blk = lambda part: pl.BlockSpec((1, LANE, width), lambda i, h, c: (i, cidx(c), part * groups + h))
    per_head = pl.BlockSpec((HGRN_HEADS, 1, LANE), lambda i, h, c: (0, 0, 0))
    const = pl.BlockSpec((LANE, LANE), lambda i, h, c: (0, 0))
    in_specs = [blk(0), blk(2 if backward else 1), blk(3), per_head, const, const]
    args = [hg, hg, hg, lb, tri, lvl]
    if finalize:
        in_specs += [blk(4), blk(0), per_head]
        args += [hg, o_fwd, nw]
    return pl.pallas_call(
        functools.partial(_gla_body, backward, finalize), grid=(b, groups, nc),
        in_specs=in_specs, out_specs=blk(0),
        out_shape=jax.ShapeDtypeStruct((b, p, HGRN_WIDTH), F32),
        scratch_shapes=[pltpu.VMEM((hp, LANE, LANE), F32), pltpu.VMEM((hp, LANE, LANE), F32)],
        compiler_params=_cparams(("parallel", "parallel", "arbitrary")),
        name="gla_bwd" if backward else "gla_fwd")(*args)


def _conv_body(cur_ref, prev_ref, next_ref, cw_ref, cb_ref, xs_ref, bc_ref, ext_ref):
    j = pl.program_id(1)
    tp = cur_ref.shape[1]
    half = SSM_CONV // 2
    ext_ref[0:SUBLANE, :] = jnp.where(j > 0, prev_ref[0], 0.0)
    ext_ref[SUBLANE:SUBLANE + tp, :] = cur_ref[0]
    ext_ref[SUBLANE + tp:2 * SUBLANE + tp, :] = jnp.where(j < pl.num_programs(1) - 1, next_ref[0], 0.0)
    acc = jnp.broadcast_to(cb_ref[...], (tp, SSM_CONV_DIM))
    for w in range(SSM_CONV):
        acc = acc + ext_ref[pl.ds(SUBLANE - half + w, tp), :] * cw_ref[w:w + 1, :]
    y = _silu(acc)
    xs_ref[0] = y[:, :SSM_WIDTH]
    bc_ref[0] = y[:, SSM_WIDTH:]


def _ssd_conv(xbc, cw, cb):
    b, p, _ = xbc.shape
    tp = _row_tile(p)
    r8 = tp // SUBLANE
    last8 = p // SUBLANE - 1
    return pl.pallas_call(
        _conv_body, grid=(b, p // tp),
        in_specs=[pl.BlockSpec((1, tp, SSM_CONV_DIM), lambda i, j: (i, j, 0)),
                  pl.BlockSpec((1, SUBLANE, SSM_CONV_DIM), lambda i, j: (i, jnp.maximum(j * r8 - 1, 0), 0)),
                  pl.BlockSpec((1, SUBLANE, SSM_CONV_DIM), lambda i, j: (i, jnp.minimum((j + 1) * r8, last8), 0)),
                  pl.BlockSpec(cw.shape, lambda i, j: (0, 0)),
                  pl.BlockSpec(cb.shape, lambda i, j: (0, 0))],
        out_specs=[pl.BlockSpec((1, tp, SSM_WIDTH), lambda i, j: (i, j, 0)),
                   pl.BlockSpec((1, tp, 2 * SSM_GROUPS * SSM_STATE), lambda i, j: (i, j, 0))],
        out_shape=[jax.ShapeDtypeStruct((b, p, SSM_WIDTH), F32),
                   jax.ShapeDtypeStruct((b, p, 2 * SSM_GROUPS * SSM_STATE), F32)],
        scratch_shapes=[pltpu.VMEM((tp + 2 * SUBLANE, SSM_CONV_DIM), F32)],
        compiler_params=_cparams(("parallel", "parallel")), name="ssd_conv")(xbc, xbc, xbc, cw, cb)


def _ssd_body(xsf_ref, bcf_ref, dtf_ref, xsb_ref, bcb_ref, dtr_ref, dtb_ref, a_ref, lo_ref, up_ref,
              yf_ref, yb_ref, stf_ref, stb_ref):
    @pl.when(pl.program_id(1) == 0)
    def _():
        stf_ref[...] = jnp.zeros_like(stf_ref)
        stb_ref[...] = jnp.zeros_like(stb_ref)

    _ssd_direction(False, xsf_ref, bcf_ref, dtf_ref, dtb_ref, a_ref, lo_ref, up_ref, yf_ref, stf_ref)
    _ssd_direction(True, xsb_ref, bcb_ref, dtr_ref, dtb_ref, a_ref, up_ref, lo_ref, yb_ref, stb_ref)


def _ssd_direction(backward, xs_ref, bc_ref, dt_ref, dtb_ref, a_ref, tri_ref, trit_ref, y_ref, st_ref):
    c = pl.program_id(1)
    cc = pl.num_programs(1) - 1 - c if backward else c
    n = LANE
    row = lax.broadcasted_iota(I32, (n, n), 0)
    col = lax.broadcasted_iota(I32, (n, n), 1)
    valid = (cc * n + row) >= PADF
    tri = (col >= row) if backward else (col <= row)

    dt = jnp.where(valid, jax.nn.softplus(dt_ref[0] + dtb_ref[...]), 0.0)
    g = dt * a_ref[...]
    cum_col = _dot_const_lhs(tri_ref[...], g, 3)
    g_t = g.T
    dt_t = dt.T
    cum_row = _dot_const_rhs(g_t, trit_ref[...], 3)
    edge_t = 0 if backward else n - 1

    bc = bc_ref[0]
    b_t = bc[:, :SSM_GROUPS * SSM_STATE].T
    cm = bc[:, SSM_GROUPS * SSM_STATE:]
    xs = xs_ref[0]
    outs = []
    for grp in range(SSM_GROUPS):
        c_g = cm[:, grp * SSM_STATE:(grp + 1) * SSM_STATE].astype(BF16)
        bt_g = b_t[grp * SSM_STATE:(grp + 1) * SSM_STATE, :]
        cb = _dot(c_g, bt_g.astype(BF16))
        for hh in range(SSM_REP):
            head = grp * SSM_REP + hh
            ln = (SSM_HEADS if backward else 0) + head
            colv = cum_col[:, ln:ln + 1]
            rowv = cum_row[ln:ln + 1, :]
            dtrow = dt_t[ln:ln + 1, :]
            diff = colv - rowv
            decay = jnp.where(tri, jnp.exp(jnp.where(tri, diff, 0.0)), 0.0)
            att = cb * (decay * dtrow)
            x_h = xs[:, head * SSM_HEAD_DIM:(head + 1) * SSM_HEAD_DIM].astype(BF16)
            st = st_ref[head]
            o = _dot(att.astype(BF16), x_h) + _dot(c_g, st.astype(BF16)) * jnp.exp(colv)
            edge = rowv[:, edge_t:edge_t + 1]
            wrow = dtrow * jnp.exp(edge - rowv)
            st_ref[head] = jnp.exp(edge) * st + _dot((bt_g * wrow).astype(BF16), x_h)
            outs.append(o)
    y_ref[0] = jnp.concatenate(outs, axis=-1)


def _ssd(xs, bc, dtp, dtb, arow, tri_lo, tri_up):
    b, p, _ = xs.shape
    nc = p // LANE
    fwd = lambda w: pl.BlockSpec((1, LANE, w), lambda i, c: (i, c, 0))
    bwd = lambda w: pl.BlockSpec((1, LANE, w), lambda i, c: (i, nc - 1 - c, 0))
    const = lambda a: pl.BlockSpec(a.shape, lambda i, c: (0, 0))
    widths = (SSM_WIDTH, 2 * SSM_GROUPS * SSM_STATE, LANE)
    state = pltpu.VMEM((SSM_HEADS, SSM_STATE, SSM_HEAD_DIM), F32)
    return pl.pallas_call(
        _ssd_body, grid=(b, nc),
        in_specs=[fwd(w) for w in widths] + [bwd(w) for w in widths]
        + [const(dtb), const(arow), const(tri_lo), const(tri_up)],
        out_specs=[fwd(SSM_WIDTH), bwd(SSM_WIDTH)],
        out_shape=[jax.ShapeDtypeStruct((b, p, SSM_WIDTH), F32)] * 2,
        scratch_shapes=[state, state],
        compiler_params=_cparams(("parallel", "arbitrary")),
        name="ssd")(xs, bc, dtp, xs, bc, dtp, dtb, arow, tri_lo, tri_up)


def _outproj_body(att_ref, rec_ref, yf_ref, yb_ref, xs_ref, z_ref, h_ref, anw_ref, snw_ref, dsk_ref,
                  wa_ref, wr_ref, ws_ref, o_ref):
    tm = h_ref.shape[1]
    j = pl.program_id(1)

    def rms(a, w_ref):
        ms = jnp.mean(a * a, axis=-1, keepdims=True)
        return a * lax.rsqrt(ms + EPS) * w_ref[...]

    att = rms(att_ref[0], anw_ref)
    y = yf_ref[0] + yb_ref[0] + dsk_ref[...] * xs_ref[0]
    ssm = rms(y * _silu(z_ref[0]), snw_ref)
    out = (_dot(att.astype(BF16), wa_ref[...]) + _dot(rec_ref[0].astype(BF16), wr_ref[...])
           + _dot(ssm.astype(BF16), ws_ref[...]))
    rowi = j * tm + lax.broadcasted_iota(I32, (tm, 1), 0)
    o_ref[0] = jnp.where(rowi >= PADF, h_ref[0] + out, 0.0)


def _outproj(att, rec, yf, yb, xs, z, h, anw, snw, dsk, wa, wr, ws):
    b, p, _ = h.shape
    tm = _row_tile(p)
    blk = lambda w: pl.BlockSpec((1, tm, w), lambda i, j: (i, j, 0))
    const = lambda a: pl.BlockSpec(a.shape, lambda i, j: (0, 0))
    return pl.pallas_call(
        _outproj_body, grid=(b, p // tm),
        in_specs=[blk(512)] * 6 + [blk(D_MODEL), const(anw), const(snw), const(dsk),
                                   const(wa), const(wr), const(ws)],
        out_specs=blk(D_MODEL), out_shape=jax.ShapeDtypeStruct((b, p, D_MODEL), F32),
        compiler_params=_cparams(("parallel", "parallel")), name="outproj")(
            att, rec, yf, yb, xs, z, h, anw, snw, dsk, wa, wr, ws)


def _router_body(h_ref, nw_ref, wh_ref, wl_ref, xe_ref, afft_ref):
    tp = h_ref.shape[1]
    j = pl.program_id(1)
    h = h_ref[0]
    ms = jnp.mean(h * h, axis=-1, keepdims=True)
    xn = h * lax.rsqrt(ms + EPS) * nw_ref[...]
    x_hi = xn.astype(BF16)
    x_lo = (xn - x_hi.astype(F32)).astype(BF16)
    logits = _dot(x_hi, wh_ref[...]) + _dot(x_hi, wl_ref[...]) + _dot(x_lo, wh_ref[...])
    lane = lax.broadcasted_iota(I32, (tp, LANE), 1)
    real = lane < N_EXPERTS
    m = jnp.max(jnp.where(real, logits, NEG_BIG), axis=-1, keepdims=True)
    pexp = jnp.exp(jnp.where(real, logits - m, 0.0))
    aff = pexp / jnp.sum(jnp.where(real, pexp, 0.0), axis=-1, keepdims=True)
    x_f = x_hi.astype(F32)
    for a in range(D_MODEL // LANE):
        xe_ref[0, pl.ds(a, tp, stride=SUBLANE), :] = x_f[:, a * LANE:(a + 1) * LANE]
    rowi = j * tp + lax.broadcasted_iota(I32, (tp, LANE), 0)
    afft_ref[0] = jnp.where(rowi >= PADF, aff, -1.0).T[:N_EXPERTS, :]


def _router(h, nw, wh, wl):
    b, p, _ = h.shape
    tp = _row_tile(p)
    const = lambda a: pl.BlockSpec(a.shape, lambda i, j: (0, 0))
    return pl.pallas_call(
        _router_body, grid=(b, p // tp),
        in_specs=[pl.BlockSpec((1, tp, D_MODEL), lambda i, j: (i, j, 0)), const(nw), const(wh), const(wl)],
        out_specs=[pl.BlockSpec((1, tp * SUBLANE, LANE), lambda i, j: (i, j, 0)),
                   pl.BlockSpec((1, N_EXPERTS, tp), lambda i, j: (i, 0, j))],
        out_shape=[jax.ShapeDtypeStruct((b, p * SUBLANE, LANE), F32),
                   jax.ShapeDtypeStruct((b, N_EXPERTS, p), F32)],
        compiler_params=_cparams(("parallel", "parallel")), name="router")(h, nw, wh, wl)


def _select_body(cap, a2_ref, a3_ref, us_ref, ls_ref, li_ref, idx_ref, gate_ref, thr_ref, need_ref):
    n = LANE
    cps = idx_ref.shape[2]
    a2 = a2_ref[0]
    bits = lax.bitcast_convert_type(a2, I32)

    def search(i, t):
        cand = t | jnp.left_shift(jnp.int32(1), 30 - i)
        cnt = jnp.sum(jnp.where(bits >= cand, 1.0, 0.0), axis=1, keepdims=True)
        return jnp.where(cnt >= cap, cand, t)

    thr = lax.fori_loop(0, 31, search, jnp.zeros((N_EXPERTS, 1), I32))
    n_gt = jnp.sum(jnp.where(bits > thr, 1.0, 0.0), axis=1, keepdims=True)
    thr_ref[...] = jnp.broadcast_to(lax.bitcast_convert_type(thr, F32), (N_EXPERTS, n))
    need_ref[...] = jnp.broadcast_to(cap - n_gt, (N_EXPERTS, n))

    ones = jnp.ones((n, n), BF16)
    us = us_ref[...]
    ls = ls_ref[...]
    li = li_ref[...]
    c_row = lax.broadcasted_iota(I32, (1, cps), 1).astype(F32)
    c_full = lax.broadcasted_iota(I32, (n, cps), 1).astype(F32)
    k_full = lax.broadcasted_iota(I32, (n, cps), 0).astype(F32)
    reps = cps // n

    def per_expert(e, carry):
        a = a3_ref[0, e]
        vt = jnp.broadcast_to(thr_ref[pl.ds(e, 1), :], (n, n))
        nd = jnp.broadcast_to(need_ref[pl.ds(e, 1), :], (n, n))
        gt = jnp.where(a > vt, 1.0, 0.0)
        eq = jnp.where(a == vt, 1.0, 0.0).astype(BF16)
        eq_before = _dot(eq, us) + _dot(ls, _dot(eq, ones).astype(BF16))
        sel = gt + jnp.where(eq_before < nd, eq.astype(F32), 0.0)
        selb = sel.astype(BF16)
        tile_cnt = _dot(selb, ones)
        off_lo = _dot(ls, tile_cnt.astype(BF16))
        off_hi = off_lo + tile_cnt
        lo = jnp.concatenate([off_lo] * reps, axis=1)
        hi = jnp.concatenate([off_hi] * reps, axis=1)
        in_tile = jnp.where(lo <= c_full, jnp.where(c_full < hi, 1.0, 0.0), 0.0)
        incl = _dot(li, sel.T.astype(BF16))
        cnt_at = _dot(incl.astype(BF16), in_tile.astype(BF16))
        tile_of = jnp.sum(in_tile * k_full, axis=0, keepdims=True)
        rank = c_row - jnp.sum(in_tile * lo, axis=0, keepdims=True)
        local = jnp.sum(jnp.where(cnt_at <= rank, 1.0, 0.0), axis=0, keepdims=True)
        idx = jnp.where(c_row < cap, tile_of * n + local, 0.0)
        idx_ref[0, pl.ds(e, 1), :] = idx.astype(I32)
        a_rows = _dot_const_rhs(a.T, in_tile.astype(BF16), 3)
        gate = jnp.sum(jnp.where(k_full == local, a_rows, 0.0), axis=0, keepdims=True)
        gate_ref[0, pl.ds(e, 1), :] = jnp.where(c_row < cap, gate, 0.0)
        return carry

    lax.fori_loop(0, N_EXPERTS, per_expert, 0)


def _select(afft, a3, us, ls, li, cap, cps):
    b, _, p = afft.shape
    const = lambda a: pl.BlockSpec(a.shape, lambda i: (0, 0))
    return pl.pallas_call(
        functools.partial(_select_body, cap), grid=(b,),
        in_specs=[pl.BlockSpec((1, N_EXPERTS, p), lambda i: (i, 0, 0)),
                  pl.BlockSpec((1, N_EXPERTS, LANE, LANE), lambda i: (i, 0, 0, 0)),
                  const(us), const(ls), const(li)],
        out_specs=[pl.BlockSpec((1, N_EXPERTS, cps), lambda i: (i, 0, 0)),
                   pl.BlockSpec((1, N_EXPERTS, cps), lambda i: (i, 0, 0))],
        out_shape=[jax.ShapeDtypeStruct((b, N_EXPERTS, cps), I32),
                   jax.ShapeDtypeStruct((b, N_EXPERTS, cps), F32)],
        scratch_shapes=[pltpu.VMEM((N_EXPERTS, LANE), F32), pltpu.VMEM((N_EXPERTS, LANE), F32)],
        compiler_params=_cparams(("parallel",)), name="select")(afft, a3, us, ls, li)


def _ffn_body(nf, ntok, idx_ref, gate_ref, xe_ref, wg_ref, wu_ref, wd_ref, y_ref,
              xs_ref, xb_ref, acc_ref, acc3_ref, sem):
    b = pl.program_id(0)
    e = pl.program_id(1)
    f = pl.program_id(2)
    ne = pl.num_programs(1)
    rows = xb_ref.shape[0]
    cps = idx_ref.shape[3] // N_EXPERTS
    chunk = rows // nf
    slot = lax.rem(e, 2)
    tok0 = b * ntok
    tile = lambda r: pl.ds(pl.multiple_of(r * SUBLANE, SUBLANE), SUBLANE)

    def row_copy(c, token, buf):
        return pltpu.make_async_copy(xe_ref.at[tile(tok0 + token), :], xs_ref.at[buf, tile(c), :], sem.at[buf])

    def drain(buf):
        def body(c, carry):
            row_copy(c, 0, buf).wait()
            return carry
        lax.fori_loop(0, rows, body, 0, unroll=8)

    @pl.when((e == 0) & (f == 0))
    def _():
        y_ref[...] = jnp.zeros_like(y_ref)

        def issue(c, carry):
            row_copy(c, idx_ref[0, 0, 0, c], 0).start()
            return carry

        lax.fori_loop(0, rows, issue, 0, unroll=8)

    @pl.when(f == 0)
    def _():
        drain(slot)
        for a in range(D_MODEL // LANE):
            xb_ref[:, a * LANE:(a + 1) * LANE] = xs_ref[slot, pl.ds(a, rows, stride=SUBLANE), :].astype(BF16)

    nxt0 = lax.rem(e + 1, ne) * cps + f * chunk
    for u in range(chunk):
        row_copy(f * chunk + u, idx_ref[0, 0, 0, nxt0 + u], 1 - slot).start()

    x = xb_ref[...]
    a = _dot(x, wg_ref[0, 0])
    u = _dot(x, wu_ref[0, 0])
    part = _dot((_silu(a) * u).astype(BF16), wd_ref[0, 0])

    @pl.when(f == 0)
    def _():
        acc_ref[...] = part

    @pl.when(f > 0)
    def _():
        acc_ref[...] += part

    @pl.when(f == nf - 1)
    def _():
        for a in range(D_MODEL // LANE):
            acc3_ref[pl.ds(a, rows, stride=SUBLANE), :] = acc_ref[:, a * LANE:(a + 1) * LANE]

        def scatter(grp, carry):
            base = grp * SCATTER_GROUP
            off = e * cps + base
            ts = [idx_ref[0, 0, 0, off + u] for u in range(SCATTER_GROUP)]
            vals = [y_ref[0, tile(ts[u]), :] + gate_ref[0, 0, 0, off + u] * acc3_ref[tile(base + u), :]
                    for u in range(SCATTER_GROUP)]
            for u in range(SCATTER_GROUP):
                y_ref[0, tile(ts[u]), :] = vals[u]
            return carry

        lax.fori_loop(0, rows // SCATTER_GROUP, scatter, 0)

    @pl.when((e == ne - 1) & (f == nf - 1))
    def _():
        drain(1 - slot)


def _ffn(idx, gate, xe, wg, wu, wd, layer, rows):
    b, p8, _ = xe.shape
    flat = lambda a: a.reshape(b, 1, 1, -1)
    tf = 2048
    nf = EXPERT_FF // tf
    assert rows % nf == 0 and rows % SCATTER_GROUP == 0 and N_EXPERTS % 2 == 0
    smem = pl.BlockSpec((1, 1, 1, idx.shape[1] * idx.shape[2]), lambda i, e, f: (i, 0, 0, 0),
                        memory_space=pltpu.SMEM)
    return pl.pallas_call(
        functools.partial(_ffn_body, nf, p8 // SUBLANE), grid=(b, N_EXPERTS, nf),
        in_specs=[smem, smem, pl.BlockSpec(memory_space=pl.ANY),
                  pl.BlockSpec((1, 1, D_MODEL, tf), lambda i, e, f: (layer, e, 0, f)),
                  pl.BlockSpec((1, 1, D_MODEL, tf), lambda i, e, f: (layer, e, 0, f)),
                  pl.BlockSpec((1, 1, tf, D_MODEL), lambda i, e, f: (layer, e, f, 0))],
        out_specs=pl.BlockSpec((1, p8, LANE), lambda i, e, f: (i, 0, 0), pipeline_mode=pl.Buffered(1)),
        out_shape=jax.ShapeDtypeStruct((b, p8, LANE), F32),
        scratch_shapes=[pltpu.VMEM((2, rows * SUBLANE, LANE), F32), pltpu.VMEM((rows, D_MODEL), BF16),
                        pltpu.VMEM((rows, D_MODEL), F32), pltpu.VMEM((rows * SUBLANE, LANE), F32),
                        pltpu.SemaphoreType.DMA((2,))],
        compiler_params=_cparams(("arbitrary", "arbitrary", "arbitrary")), name="expert_ffn")(
            flat(idx), flat(gate), xe.reshape(b * p8, LANE), wg, wu, wd)


def _final_body(h_ref, y_ref, o_ref):
    o_ref[0] = h_ref[0] + _untile_rows(y_ref.at[0], LANE)


def _final_add(h, y, seq):
    b, p, _ = h.shape
    lead_blocks = LEAD // LANE
    src = pl.BlockSpec((1, LANE, D_MODEL), lambda i, j: (i, j + lead_blocks, 0))
    return pl.pallas_call(
        _final_body, grid=(b, seq // LANE),
        in_specs=[src, pl.BlockSpec((1, LANE * SUBLANE, LANE), lambda i, j: (i, j + lead_blocks, 0))],
        out_specs=pl.BlockSpec((1, LANE, D_MODEL), lambda i, j: (i, j, 0)),
        out_shape=jax.ShapeDtypeStruct((b, seq, D_MODEL), F32),
        compiler_params=_cparams(("parallel", "parallel")), name="final_add")(h, y)


def _rope_tables(seq):
    rows = seq // GRID_W
    row = jnp.repeat(jnp.arange(rows), GRID_W).astype(F32)
    col = (jnp.arange(rows * GRID_W) % GRID_W).astype(F32)
    n_pair = HEAD_DIM // 4
    inv = ROPE_THETA ** (-jnp.arange(n_pair, dtype=F32) / n_pair)
    ang = jnp.concatenate([row[:, None] * inv, col[:, None] * inv], axis=-1)
    ang = jnp.concatenate([jnp.zeros((LEAD, HEAD_DIM // 2), F32), ang], axis=0)
    cos, sin = jnp.cos(ang), jnp.sin(ang)
    cos64 = jnp.concatenate([cos, cos], axis=-1)
    sin64 = jnp.concatenate([-sin, sin], axis=-1)
    scale = HEAD_DIM ** -0.5
    return (jnp.tile(cos64, (1, ATT_HEADS)) * scale, jnp.tile(sin64, (1, ATT_HEADS)) * scale,
            jnp.tile(cos64, (1, ATT_KV_HEADS)), jnp.tile(sin64, (1, ATT_KV_HEADS)))


def _block_mean_matrix(width):
    i = jnp.arange(width)
    return jnp.where((i[:, None] // HEAD_DIM) == (i[None, :] // HEAD_DIM), 1.0 / HEAD_DIM, 0.0).astype(BF16)


def kernel(x, meta_tokens, norm1_w, w_in, q_norm_w, k_norm_w, attn_norm_w, hgrn_lb, hgrn_norm_w, conv_w, conv_b, dt_bias, a_log, d_skip, ssm_norm_w, w_out, norm2_w, router_w, w_gate, w_up, w_down):
    bsz, seq, _ = x.shape
    depth = norm1_w.shape[0]
    assert seq % LANE == 0 and seq % GRID_W == 0 and D_MODEL == SUBLANE * LANE
    p = LEAD + seq
    assert p // LANE <= LANE
    n_tok = N_META + seq
    cap = CAPACITY_FACTOR * n_tok // N_EXPERTS
    rows_ffn = -(-cap // 16) * 16
    cps = -(-cap // LANE) * LANE

    h = jnp.concatenate([jnp.zeros((bsz, PADF, D_MODEL), x.dtype),
                         jnp.broadcast_to(meta_tokens[None].astype(x.dtype), (bsz, N_META, D_MODEL)), x], axis=1)

    tabs = _rope_tables(seq)
    tok = jnp.arange(p)
    att_bias = jnp.where(tok[:LEAD] >= PADF, 0.0, NEG_BIG).astype(F32)[None, :]
    gq = _block_mean_matrix(ATT_WIDTH)
    gk = _block_mean_matrix(KV_WIDTH)
    ii = jnp.arange(LANE)
    tri_lo = (ii[None, :] <= ii[:, None]).astype(BF16)
    tri_up = (ii[None, :] >= ii[:, None]).astype(BF16)
    strict_up = (ii[:, None] < ii[None, :]).astype(BF16)
    strict_lo = (ii[None, :] < ii[:, None]).astype(BF16)

    xor = ii[:, None] ^ ii[None, :]
    msb = jnp.zeros_like(xor)
    for bit in range(7):
        msb = jnp.where(xor >= (1 << bit), 1 << bit, msb)
    lvl_f = jnp.where(ii[:, None] >= ii[None, :], msb, -1).astype(I32)
    lvl_b = jnp.where(ii[:, None] <= ii[None, :], msb, -1).astype(I32)

    soft = jax.nn.softmax(hgrn_lb.astype(F32), axis=1)
    lower_bounds = jnp.clip(jnp.cumsum(soft, axis=1) - soft[:, :1], 0.0, 1.0)

    wg_all, wu_all, wd_all = w_gate.astype(BF16), w_up.astype(BF16), w_down.astype(BF16)
    y = None
    for l in range(depth):
        w = w_in[l]
        o = 0
        pieces = []
        for width in (ATT_WIDTH + 2 * KV_WIDTH, 5 * HGRN_WIDTH, SSM_WIDTH, SSM_CONV_DIM, 2 * SSM_HEADS):
            pieces.append(w[:, o:o + width])
            o += width
        pieces[4] = jnp.pad(pieces[4], ((0, 0), (0, LANE - 2 * SSM_HEADS)))
        ws = [a.astype(BF16) for a in pieces]
        h2, (att_p, hg_p, z_p, xbc_p, dt_p) = _inproj(
            h.reshape(bsz * p, D_MODEL), None if y is None else y.reshape(bsz * p * SUBLANE, LANE),
            norm1_w[l][None, :], ws)
        h = h2.reshape(bsz, p, D_MODEL)
        r3 = lambda a: a.reshape(bsz, p, a.shape[-1])
        att_p, hg_p, z_p, xbc_p, dt_p = r3(att_p), r3(hg_p), r3(z_p), r3(xbc_p), r3(dt_p)

        qr, kt, vv = _attn_prep(att_p, tabs, jnp.tile(q_norm_w[l], ATT_HEADS)[None, :],
                                jnp.tile(k_norm_w[l], ATT_KV_HEADS)[None, :], gq, gk)
        att = _attention(qr, kt, vv, att_bias)

        lb_f = lower_bounds[0, l].reshape(HGRN_HEADS, 1, HGRN_DK)
        lb_b = lower_bounds[1, l].reshape(HGRN_HEADS, 1, HGRN_DK)
        o_f = _gla(hg_p, lb_f, tri_lo, lvl_f, False)
        rec = _gla(hg_p, lb_b, tri_up, lvl_b, True, o_f, hgrn_norm_w[l].reshape(HGRN_HEADS, 1, HGRN_DK))

        cw = jnp.pad(conv_w[l].T, ((0, SUBLANE - SSM_CONV), (0, 0)))
        xs, bc = _ssd_conv(xbc_p, cw, conv_b[l][None, :])
        dtb = jnp.pad(dt_bias[l].reshape(-1), (0, LANE - 2 * SSM_HEADS))[None, :]
        arow = jnp.pad(-jnp.exp(a_log[l].astype(F32)).reshape(-1), (0, LANE - 2 * SSM_HEADS))[None, :]
        y_f, y_b = _ssd(xs, bc, dt_p, dtb, arow, tri_lo, tri_up)

        wo = w_out[l].astype(BF16)
        h = _outproj(att, rec, y_f, y_b, xs, z_p, h, attn_norm_w[l][None, :], ssm_norm_w[l][None, :],
                     jnp.repeat(d_skip[l], SSM_HEAD_DIM)[None, :],
                     wo[:ATT_WIDTH], wo[ATT_WIDTH:ATT_WIDTH + HGRN_WIDTH], wo[ATT_WIDTH + HGRN_WIDTH:])

        wr = jnp.pad(router_w[l], ((0, 0), (0, LANE - N_EXPERTS)))
        wr_hi = wr.astype(BF16)
        wr_lo = (wr - wr_hi.astype(F32)).astype(BF16)
        xe, afft = _router(h, norm2_w[l][None, :], wr_hi, wr_lo)
        nb = p // LANE
        a3 = jnp.pad(afft.reshape(bsz, N_EXPERTS, nb, LANE), ((0, 0), (0, 0), (0, LANE - nb), (0, 0)),
                     constant_values=-1.0)
        idx, gate = _select(afft, a3, strict_up, strict_lo, tri_lo, cap, cps)
        y = _ffn(idx, gate, xe, wg_all, wu_all, wd_all, l, rows_ffn)

    return _final_add(h, y, seq)
```

```python
import functools

import jax
import jax.numpy as jnp
from jax import lax
from jax.experimental import pallas as pl
from jax.experimental.pallas import tpu as pltpu

F32 = jnp.float32
BF16 = jnp.bfloat16
I32 = jnp.int32

D_MODEL = 1024
N_META = 16
GRID_W = 64
ROPE_THETA = 10000.0
EPS = 1e-6
EXP_CLIP = 30.0
LOG2E = 1.4426950408889634

ATT_HEADS = 8
ATT_KV_HEADS = 2
HEAD_DIM = 64
ATT_WIDTH = ATT_HEADS * HEAD_DIM
KV_WIDTH = ATT_KV_HEADS * HEAD_DIM
ATT_REP = ATT_HEADS // ATT_KV_HEADS

HGRN_HEADS = 4
HGRN_DK = 128
HGRN_WIDTH = 512

SSM_HEADS = 8
SSM_HEAD_DIM = 64
SSM_GROUPS = 2
SSM_STATE = 64
SSM_CONV = 7
SSM_WIDTH = 512
SSM_CONV_DIM = 768
SSM_REP = SSM_HEADS // SSM_GROUPS

N_EXPERTS = 16
EXPERT_FF = 2048
CAPACITY_FACTOR = 2

LANE = 128
SUBLANE = 8
LEAD = 128
PADF = LEAD - N_META
NEG_BIG = -1e30
VMEM_LIMIT = 60 * 1024 * 1024
SCATTER_GROUP = 24
GLA_HEADS_PER_STEP = 4


def _cparams(sem):
    return pltpu.CompilerParams(dimension_semantics=sem, vmem_limit_bytes=VMEM_LIMIT)


def _dot(a, b):
    return jnp.dot(a, b, preferred_element_type=F32)


def _dot_nt(a, b):
    return lax.dot_general(a, b, (((1,), (1,)), ((), ())), preferred_element_type=F32)


def _split_bf16(x, n):
    parts, r = [], x
    for _ in range(n):
        p = r.astype(BF16)
        parts.append(p)
        r = r - p.astype(F32)
    return parts


def _dot_const_lhs(c, x, n=3):
    out = None
    for p in _split_bf16(x, n):
        t = _dot(c, p)
        out = t if out is None else out + t
    return out


def _dot_const_rhs(x, c, n=3):
    out = None
    for p in _split_bf16(x, n):
        t = _dot(p, c)
        out = t if out is None else out + t
    return out


def _silu(x):
    return x * jax.nn.sigmoid(x)


def _row_tile(p):
    return 384 if p % 384 == 0 else LANE


def _untile_rows(y_ref, rows):
    return jnp.concatenate([y_ref[pl.ds(a, rows, stride=SUBLANE), :] for a in range(D_MODEL // LANE)], axis=-1)


def _inproj_body(has_y, *refs):
    if has_y:
        h_ref, y_ref, nw_ref = refs[:3]
        ws = refs[3:8]
        hn_ref = refs[8]
        outs = refs[9:14]
        h = h_ref[...] + _untile_rows(y_ref, h_ref.shape[0])
        hn_ref[...] = h
    else:
        h_ref, nw_ref = refs[:2]
        ws = refs[2:7]
        outs = refs[7:12]
        h = h_ref[...]
    ms = jnp.mean(h * h, axis=-1, keepdims=True)
    u = (h * lax.rsqrt(ms + EPS) * nw_ref[...]).astype(BF16)
    for w_ref, o_ref in zip(ws, outs):
        o_ref[...] = _dot(u, w_ref[...])


def _inproj(h2, y2, nw, ws):
    m = h2.shape[0]
    tm = 256 if m % 256 == 0 else LANE
    has_y = y2 is not None
    row = pl.BlockSpec((tm, D_MODEL), lambda i: (i, 0))
    const = lambda a: pl.BlockSpec(a.shape, lambda i: (0, 0))
    yrow = pl.BlockSpec((tm * SUBLANE, LANE), lambda i: (i, 0))
    in_specs = [row] + ([yrow] if has_y else []) + [const(nw)] + [const(w) for w in ws]
    out_shape = ([jax.ShapeDtypeStruct((m, D_MODEL), F32)] if has_y else []) + [
        jax.ShapeDtypeStruct((m, w.shape[1]), F32) for w in ws]
    out_specs = ([row] if has_y else []) + [pl.BlockSpec((tm, w.shape[1]), lambda i: (i, 0)) for w in ws]
    args = [h2] + ([y2] if has_y else []) + [nw] + list(ws)
    res = pl.pallas_call(
        functools.partial(_inproj_body, has_y),
        grid=(m // tm,), in_specs=in_specs, out_specs=out_specs, out_shape=out_shape,
        compiler_params=_cparams(("parallel",)), name="inproj")(*args)
    if has_y:
        return res[0], res[1:]
    return h2, res


def _attn_prep_body(x_ref, cq_ref, sq_ref, ck_ref, sk_ref, qw_ref, kw_ref, gq_ref, gk_ref,
                    q_ref, kt_ref, v_ref):
    x = x_ref[0]
    q = x[:, :ATT_WIDTH]
    k = x[:, ATT_WIDTH:ATT_WIDTH + KV_WIDTH]
    v = x[:, ATT_WIDTH + KV_WIDTH:]

    def head_norm(a, g_ref, w_ref):
        ms = _dot_const_rhs(a * a, g_ref[...], 2)
        return a * lax.rsqrt(ms + EPS) * w_ref[...]

    def rope(a, c_ref, s_ref):
        width = a.shape[1]
        lane = lax.broadcasted_iota(I32, a.shape, 1)
        first = (lane & (HEAD_DIM - 1)) < HEAD_DIM // 2
        partner = jnp.where(first, pltpu.roll(a, width - HEAD_DIM // 2, 1),
                            pltpu.roll(a, HEAD_DIM // 2, 1))
        return a * c_ref[...] + partner * s_ref[...]

    q_ref[0] = rope(head_norm(q, gq_ref, qw_ref), cq_ref, sq_ref).astype(BF16)
    kr = rope(head_norm(k, gk_ref, kw_ref), ck_ref, sk_ref)
    kt_ref[0] = kr.T.astype(BF16)
    ones = jnp.ones((v.shape[0], HEAD_DIM), F32)
    for g in range(ATT_KV_HEADS):
        v_ref[0, g] = jnp.concatenate([v[:, g * HEAD_DIM:(g + 1) * HEAD_DIM], ones], axis=-1).astype(BF16)


def _attn_prep(att_p, tabs, qw, kw, gq, gk):
    b, p, _ = att_p.shape
    tp = _row_tile(p)
    cq, sq, ck, sk = tabs
    tab = lambda a: pl.BlockSpec((tp, a.shape[1]), lambda i, j: (j, 0))
    const = lambda a: pl.BlockSpec(a.shape, lambda i, j: (0, 0))
    return pl.pallas_call(
        _attn_prep_body, grid=(b, p // tp),
        in_specs=[pl.BlockSpec((1, tp, ATT_WIDTH + 2 * KV_WIDTH), lambda i, j: (i, j, 0)),
                  tab(cq), tab(sq), tab(ck), tab(sk), const(qw), const(kw), const(gq), const(gk)],
        out_specs=[pl.BlockSpec((1, tp, ATT_WIDTH), lambda i, j: (i, j, 0)),
                   pl.BlockSpec((1, KV_WIDTH, tp), lambda i, j: (i, 0, j)),
                   pl.BlockSpec((1, ATT_KV_HEADS, tp, 2 * HEAD_DIM), lambda i, j: (i, 0, j, 0))],
        out_shape=[jax.ShapeDtypeStruct((b, p, ATT_WIDTH), BF16),
                   jax.ShapeDtypeStruct((b, KV_WIDTH, p), BF16),
                   jax.ShapeDtypeStruct((b, ATT_KV_HEADS, p, 2 * HEAD_DIM), BF16)],
        compiler_params=_cparams(("parallel", "parallel")), name="attn_prep")(
            att_p, cq, sq, ck, sk, qw, kw, gq, gk)


def _attn_body(q_ref, kt_ref, v_ref, bias_ref, o_ref):
    bias = bias_ref[...]
    heads = [(g, r) for g in range(ATT_KV_HEADS) for r in range(ATT_REP)]

    def scores(g, r):
        h = g * ATT_REP + r
        qh = q_ref[0, :, h * HEAD_DIM:(h + 1) * HEAD_DIM]
        rows = slice(g * HEAD_DIM, (g + 1) * HEAD_DIM)
        return _dot(qh, kt_ref[0, rows, :LEAD]) + bias, _dot(qh, kt_ref[0, rows, LEAD:])

    outs = []
    nxt = scores(*heads[0])
    for i, (g, r) in enumerate(heads):
        s0, s1 = nxt
        if i + 1 < len(heads):
            nxt = scores(*heads[i + 1])
        m = jnp.maximum(jnp.max(s0, axis=-1, keepdims=True), jnp.max(s1, axis=-1, keepdims=True))
        p0 = jnp.exp(s0 - m).astype(BF16)
        p1 = jnp.exp(s1 - m).astype(BF16)
        ov = _dot(p0, v_ref[0, g, :LEAD, :]) + _dot(p1, v_ref[0, g, LEAD:, :])
        outs.append(ov[:, :HEAD_DIM] / ov[:, HEAD_DIM:HEAD_DIM + 1])
    o_ref[0] = jnp.concatenate(outs, axis=-1)


def _attention(q, kt, v, bias):
    b, p, _ = q.shape
    tq = _row_tile(p)
    return pl.pallas_call(
        _attn_body, grid=(b, p // tq),
        in_specs=[pl.BlockSpec((1, tq, ATT_WIDTH), lambda i, j: (i, j, 0)),
                  pl.BlockSpec((1, KV_WIDTH, p), lambda i, j: (i, 0, 0)),
                  pl.BlockSpec((1, ATT_KV_HEADS, p, 2 * HEAD_DIM), lambda i, j: (i, 0, 0, 0)),
                  pl.BlockSpec((1, LEAD), lambda i, j: (0, 0))],
        out_specs=pl.BlockSpec((1, tq, ATT_WIDTH), lambda i, j: (i, j, 0)),
        out_shape=jax.ShapeDtypeStruct((b, p, ATT_WIDTH), F32),
        compiler_params=_cparams(("parallel", "parallel")), name="attention")(q, kt, v, bias)


def _gla_body(backward, finalize, *refs):
    if finalize:
        (q_ref, f_ref, v_ref, lb_ref, tri_ref, lvl_ref, gate_ref, of_ref, nw_ref,
         o_ref, st_ref, cum_ref) = refs
    else:
        q_ref, f_ref, v_ref, lb_ref, tri_ref, lvl_ref, o_ref, st_ref, cum_ref = refs
    hgrp = pl.program_id(1)
    c = pl.program_id(2)
    cc = pl.num_programs(2) - 1 - c if backward else c

    @pl.when(c == 0)
    def _():
        st_ref[...] = jnp.zeros_like(st_ref)

    n = LANE
    row = lax.broadcasted_iota(I32, (n, n), 0)
    valid = (cc * n + row) >= PADF
    r4 = row & 3
    odd = (row & 1) != 0
    lvl = lvl_ref[...]
    tri = tri_ref[...]

    for hd in range(GLA_HEADS_PER_STEP):
        sl = slice(hd * n, (hd + 1) * n)
        x = f_ref[0, :, sl]
        lb = lb_ref[hgrp * GLA_HEADS_PER_STEP + hd]
        g = jnp.minimum(x, 0.0) - jnp.log1p(jnp.exp(-jnp.abs(x)))
        g = g + jnp.log1p(lb * jnp.exp(jnp.minimum(-x, EXP_CLIP)))
        g = jnp.where(valid, jnp.minimum(g, 0.0), 0.0)
        k = jnp.where(valid, 1.0 - jnp.exp(g), 0.0)
        g = g * LOG2E
        q = _silu(q_ref[0, :, sl])
        v = v_ref[0, :, sl]

        cum = _dot_const_lhs(tri, g, 2)
        cum_ref[hd] = cum

        q_bf = q.astype(BF16)
        k_bf = k.astype(BF16)
        att = jnp.where(lvl == 0, _dot_nt(q_bf, k_bf), 0.0)
        g_up = pltpu.roll(g, n - 1, 0)
        g_dn = pltpu.roll(g, 1, 0)
        s = 1
        while s < n:
            if s == 1:
                expo = jnp.where(odd, 0.0, g) if backward else jnp.where(odd, g, 0.0)
            elif s == 2:
                if backward:
                    expo = jnp.where(r4 == 0, g + g_up, jnp.where(r4 == 1, g, jnp.where(r4 == 2, 0.0, g_dn)))
                else:
                    expo = jnp.where(r4 == 0, g_up, jnp.where(r4 == 1, 0.0, jnp.where(r4 == 2, g, g + g_dn)))
            else:
                parts = []
                for blk in range(n // (2 * s)):
                    mid = blk * 2 * s + (s if backward else s - 1)
                    parts.append(jnp.broadcast_to(cum_ref[hd, pl.ds(mid, 1), :], (2 * s, n)))
                ref_cum = parts[0] if len(parts) == 1 else jnp.concatenate(parts, axis=0)
                expo = -jnp.abs(cum - ref_cum)
            e = jnp.exp2(expo).astype(BF16)
            a = _dot_nt(q_bf * e, k_bf * e)
            att = jnp.where(lvl == s, a, att)
            s *= 2

        edge = cum_ref[hd, pl.ds(0 if backward else n - 1, 1), :]
        st = st_ref[hd]
        o = _dot(att.astype(BF16), v.astype(BF16)) + _dot_nt((q * jnp.exp2(cum)).astype(BF16), st.astype(BF16))
        khat = k * jnp.exp2(edge - cum)
        st_ref[hd] = st * jnp.exp2(edge) + _dot(v.T.astype(BF16), khat.astype(BF16))

        if finalize:
            o = o + of_ref[0, :, sl]
            ms = jnp.mean(o * o, axis=-1, keepdims=True)
            o = o * lax.rsqrt(ms + EPS) * nw_ref[hgrp * GLA_HEADS_PER_STEP + hd]
            o = o * _silu(gate_ref[0, :, sl])
        o_ref[0, :, sl] = o


def _gla(hg, lb, tri, lvl, backward, o_fwd=None, nw=None):
    b, p, _ = hg.shape
    nc = p // LANE
    hp = GLA_HEADS_PER_STEP
    width = hp * LANE
    groups = HGRN_HEADS // hp
    finalize = o_fwd is not None
    cidx = (lambda c: nc - 1 - c) if backward else (lambda c: c)
    blk = lambda part: pl.BlockSpec((1, LANE, width), lambda i, h, c: (i, cidx(c), part * groups + h))
    per_head = pl.BlockSpec((HGRN_HEADS, 1, LANE), lambda i, h, c: (0, 0, 0))
    const = pl.BlockSpec((LANE, LANE), lambda i, h, c: (0, 0))
    in_specs = [blk(0), blk(2 if backward else 1), blk(3), per_head, const, const]
    args = [hg, hg, hg, lb, tri, lvl]
    if finalize:
        in_specs += [blk(4), blk(0), per_head]
        args += [hg, o_fwd, nw]
    return pl.pallas_call(
        functools.partial(_gla_body, backward, finalize), grid=(b, groups, nc),
        in_specs=in_specs, out_specs=blk(0),
        out_shape=jax.ShapeDtypeStruct((b, p, HGRN_WIDTH), F32),
        scratch_shapes=[pltpu.VMEM((hp, LANE, LANE), F32), pltpu.VMEM((hp, LANE, LANE), F32)],
        compiler_params=_cparams(("parallel", "parallel", "arbitrary")),
        name="gla_bwd" if backward else "gla_fwd")(*args)


def _conv_body(cur_ref, prev_ref, next_ref, cw_ref, cb_ref, xs_ref, bc_ref, ext_ref):
    j = pl.program_id(1)
    tp = cur_ref.shape[1]
    half = SSM_CONV // 2
    ext_ref[0:SUBLANE, :] = jnp.where(j > 0, prev_ref[0], 0.0)
    ext_ref[SUBLANE:SUBLANE + tp, :] = cur_ref[0]
    ext_ref[SUBLANE + tp:2 * SUBLANE + tp, :] = jnp.where(j < pl.num_programs(1) - 1, next_ref[0], 0.0)
    acc = jnp.broadcast_to(cb_ref[...], (tp, SSM_CONV_DIM))
    for w in range(SSM_CONV):
        acc = acc + ext_ref[pl.ds(SUBLANE - half + w, tp), :] * cw_ref[w:w + 1, :]
    y = _silu(acc)
    xs_ref[0] = y[:, :SSM_WIDTH]
    bc_ref[0] = y[:, SSM_WIDTH:]


def _ssd_conv(xbc, cw, cb):
    b, p, _ = xbc.shape
    tp = _row_tile(p)
    r8 = tp // SUBLANE
    last8 = p // SUBLANE - 1
    return pl.pallas_call(
        _conv_body, grid=(b, p // tp),
        in_specs=[pl.BlockSpec((1, tp, SSM_CONV_DIM), lambda i, j: (i, j, 0)),
                  pl.BlockSpec((1, SUBLANE, SSM_CONV_DIM), lambda i, j: (i, jnp.maximum(j * r8 - 1, 0), 0)),
                  pl.BlockSpec((1, SUBLANE, SSM_CONV_DIM), lambda i, j: (i, jnp.minimum((j + 1) * r8, last8), 0)),
                  pl.BlockSpec(cw.shape, lambda i, j: (0, 0)),
                  pl.BlockSpec(cb.shape, lambda i, j: (0, 0))],
        out_specs=[pl.BlockSpec((1, tp, SSM_WIDTH), lambda i, j: (i, j, 0)),
                   pl.BlockSpec((1, tp, 2 * SSM_GROUPS * SSM_STATE), lambda i, j: (i, j, 0))],
        out_shape=[jax.ShapeDtypeStruct((b, p, SSM_WIDTH), F32),
                   jax.ShapeDtypeStruct((b, p, 2 * SSM_GROUPS * SSM_STATE), F32)],
        scratch_shapes=[pltpu.VMEM((tp + 2 * SUBLANE, SSM_CONV_DIM), F32)],
        compiler_params=_cparams(("parallel", "parallel")), name="ssd_conv")(xbc, xbc, xbc, cw, cb)


def _ssd_body(xsf_ref, bcf_ref, dtf_ref, xsb_ref, bcb_ref, dtr_ref, dtb_ref, a_ref, lo_ref, up_ref,
              yf_ref, yb_ref, stf_ref, stb_ref):
    @pl.when(pl.program_id(1) == 0)
    def _():
        stf_ref[...] = jnp.zeros_like(stf_ref)
        stb_ref[...] = jnp.zeros_like(stb_ref)

    _ssd_direction(False, xsf_ref, bcf_ref, dtf_ref, dtb_ref, a_ref, lo_ref, up_ref, yf_ref, stf_ref)
    _ssd_direction(True, xsb_ref, bcb_ref, dtr_ref, dtb_ref, a_ref, up_ref, lo_ref, yb_ref, stb_ref)


def _ssd_direction(backward, xs_ref, bc_ref, dt_ref, dtb_ref, a_ref, tri_ref, trit_ref, y_ref, st_ref):
    c = pl.program_id(1)
    cc = pl.num_programs(1) - 1 - c if backward else c
    n = LANE
    row = lax.broadcasted_iota(I32, (n, n), 0)
    col = lax.broadcasted_iota(I32, (n, n), 1)
    valid = (cc * n + row) >= PADF
    tri = (col >= row) if backward else (col <= row)

    dt = jnp.where(valid, jax.nn.softplus(dt_ref[0] + dtb_ref[...]), 0.0)
    g = dt * a_ref[...]
    cum_col = _dot_const_lhs(tri_ref[...], g, 3)
    g_t = g.T
    dt_t = dt.T
    cum_row = _dot_const_rhs(g_t, trit_ref[...], 3)
    edge_t = 0 if backward else n - 1

    bc = bc_ref[0]
    b_t = bc[:, :SSM_GROUPS * SSM_STATE].T
    cm = bc[:, SSM_GROUPS * SSM_STATE:]
    xs = xs_ref[0]
    outs = []
    for grp in range(SSM_GROUPS):
        c_g = cm[:, grp * SSM_STATE:(grp + 1) * SSM_STATE].astype(BF16)
        bt_g = b_t[grp * SSM_STATE:(grp + 1) * SSM_STATE, :]
        cb = _dot(c_g, bt_g.astype(BF16))
        for hh in range(SSM_REP):
            head = grp * SSM_REP + hh
            ln = (SSM_HEADS if backward else 0) + head
            colv = cum_col[:, ln:ln + 1]
            rowv = cum_row[ln:ln + 1, :]
            dtrow = dt_t[ln:ln + 1, :]
            diff = colv - rowv
            decay = jnp.where(tri, jnp.exp(jnp.where(tri, diff, 0.0)), 0.0)
            att = cb * (decay * dtrow)
            x_h = xs[:, head * SSM_HEAD_DIM:(head + 1) * SSM_HEAD_DIM].astype(BF16)
            st = st_ref[head]
            o = _dot(att.astype(BF16), x_h) + _dot(c_g, st.astype(BF16)) * jnp.exp(colv)
            edge = rowv[:, edge_t:edge_t + 1]
            wrow = dtrow * jnp.exp(edge - rowv)
            st_ref[head] = jnp.exp(edge) * st + _dot((bt_g * wrow).astype(BF16), x_h)
            outs.append(o)
    y_ref[0] = jnp.concatenate(outs, axis=-1)


def _ssd(xs, bc, dtp, dtb, arow, tri_lo, tri_up):
    b, p, _ = xs.shape
    nc = p // LANE
    fwd = lambda w: pl.BlockSpec((1, LANE, w), lambda i, c: (i, c, 0))
    bwd = lambda w: pl.BlockSpec((1, LANE, w), lambda i, c: (i, nc - 1 - c, 0))
    const = lambda a: pl.BlockSpec(a.shape, lambda i, c: (0, 0))
    widths = (SSM_WIDTH, 2 * SSM_GROUPS * SSM_STATE, LANE)
    state = pltpu.VMEM((SSM_HEADS, SSM_STATE, SSM_HEAD_DIM), F32)
    return pl.pallas_call(
        _ssd_body, grid=(b, nc),
        in_specs=[fwd(w) for w in widths] + [bwd(w) for w in widths]
        + [const(dtb), const(arow), const(tri_lo), const(tri_up)],
        out_specs=[fwd(SSM_WIDTH), bwd(SSM_WIDTH)],
        out_shape=[jax.ShapeDtypeStruct((b, p, SSM_WIDTH), F32)] * 2,
        scratch_shapes=[state, state],
        compiler_params=_cparams(("parallel", "arbitrary")),
        name="ssd")(xs, bc, dtp, xs, bc, dtp, dtb, arow, tri_lo, tri_up)


def _outproj_body(att_ref, rec_ref, yf_ref, yb_ref, xs_ref, z_ref, h_ref, anw_ref, snw_ref, dsk_ref,
                  wa_ref, wr_ref, ws_ref, o_ref):
    tm = h_ref.shape[1]
    j = pl.program_id(1)

    def rms(a, w_ref):
        ms = jnp.mean(a * a, axis=-1, keepdims=True)
        return a * lax.rsqrt(ms + EPS) * w_ref[...]

    att = rms(att_ref[0], anw_ref)
    y = yf_ref[0] + yb_ref[0] + dsk_ref[...] * xs_ref[0]
    ssm = rms(y * _silu(z_ref[0]), snw_ref)
    out = (_dot(att.astype(BF16), wa_ref[...]) + _dot(rec_ref[0].astype(BF16), wr_ref[...])
           + _dot(ssm.astype(BF16), ws_ref[...]))
    rowi = j * tm + lax.broadcasted_iota(I32, (tm, 1), 0)
    o_ref[0] = jnp.where(rowi >= PADF, h_ref[0] + out, 0.0)


def _outproj(att, rec, yf, yb, xs, z, h, anw, snw, dsk, wa, wr, ws):
    b, p, _ = h.shape
    tm = _row_tile(p)
    blk = lambda w: pl.BlockSpec((1, tm, w), lambda i, j: (i, j, 0))
    const = lambda a: pl.BlockSpec(a.shape, lambda i, j: (0, 0))
    return pl.pallas_call(
        _outproj_body, grid=(b, p // tm),
        in_specs=[blk(512)] * 6 + [blk(D_MODEL), const(anw), const(snw), const(dsk),
                                   const(wa), const(wr), const(ws)],
        out_specs=blk(D_MODEL), out_shape=jax.ShapeDtypeStruct((b, p, D_MODEL), F32),
        compiler_params=_cparams(("parallel", "parallel")), name="outproj")(
            att, rec, yf, yb, xs, z, h, anw, snw, dsk, wa, wr, ws)


def _router_body(h_ref, nw_ref, w2_ref, xe_ref, afft_ref):
    tp = h_ref.shape[1]
    j = pl.program_id(1)
    h = h_ref[0]
    ms = jnp.mean(h * h, axis=-1, keepdims=True)
    xn = h * lax.rsqrt(ms + EPS) * nw_ref[...]
    x_hi = xn.astype(BF16)
    x_lo = (xn - x_hi.astype(F32)).astype(BF16)
    wide = _dot(x_hi, w2_ref[...])
    logits = wide[:, :LANE] + wide[:, LANE:] + _dot(x_lo, w2_ref[:, :LANE])
    lane = lax.broadcasted_iota(I32, (tp, LANE), 1)
    real = lane < N_EXPERTS
    m = jnp.max(jnp.where(real, logits, NEG_BIG), axis=-1, keepdims=True)
    pexp = jnp.exp(jnp.where(real, logits - m, 0.0))
    aff = pexp / jnp.sum(jnp.where(real, pexp, 0.0), axis=-1, keepdims=True)
    x_f = x_hi.astype(F32)
    for a in range(D_MODEL // LANE):
        xe_ref[0, pl.ds(a, tp, stride=SUBLANE), :] = x_f[:, a * LANE:(a + 1) * LANE]
    rowi = j * tp + lax.broadcasted_iota(I32, (tp, LANE), 0)
    afft_ref[0] = jnp.where(rowi >= PADF, aff, -1.0).T[:N_EXPERTS, :]


def _router(h, nw, w2):
    b, p, _ = h.shape
    tp = _row_tile(p)
    const = lambda a: pl.BlockSpec(a.shape, lambda i, j: (0, 0))
    return pl.pallas_call(
        _router_body, grid=(b, p // tp),
        in_specs=[pl.BlockSpec((1, tp, D_MODEL), lambda i, j: (i, j, 0)), const(nw), const(w2)],
        out_specs=[pl.BlockSpec((1, tp * SUBLANE, LANE), lambda i, j: (i, j, 0)),
                   pl.BlockSpec((1, N_EXPERTS, tp), lambda i, j: (i, 0, j))],
        out_shape=[jax.ShapeDtypeStruct((b, p * SUBLANE, LANE), F32),
                   jax.ShapeDtypeStruct((b, N_EXPERTS, p), F32)],
        compiler_params=_cparams(("parallel", "parallel")), name="router")(h, nw, w2)


def _select_body(cap, a2_ref, a3_ref, us_ref, ls_ref, li_ref, idx_ref, gate_ref, thr_ref, need_ref):
    n = LANE
    cps = idx_ref.shape[2]
    a2 = a2_ref[0]
    bits = lax.bitcast_convert_type(a2, I32)

    def search(i, t):
        cand = t | jnp.left_shift(jnp.int32(1), 30 - i)
        cnt = jnp.sum(jnp.where(bits >= cand, 1.0, 0.0), axis=1, keepdims=True)
        return jnp.where(cnt >= cap, cand, t)

    thr = lax.fori_loop(0, 31, search, jnp.zeros((N_EXPERTS, 1), I32))
    n_gt = jnp.sum(jnp.where(bits > thr, 1.0, 0.0), axis=1, keepdims=True)
    thr_ref[...] = jnp.broadcast_to(lax.bitcast_convert_type(thr, F32), (N_EXPERTS, n))
    need_ref[...] = jnp.broadcast_to(cap - n_gt, (N_EXPERTS, n))

    ones = jnp.ones((n, n), BF16)
    us = us_ref[...]
    ls = ls_ref[...]
    li = li_ref[...]
    c_row = lax.broadcasted_iota(I32, (1, cps), 1).astype(F32)
    c_full = lax.broadcasted_iota(I32, (n, cps), 1).astype(F32)
    k_full = lax.broadcasted_iota(I32, (n, cps), 0).astype(F32)
    reps = cps // n

    def per_expert(e, carry):
        a = a3_ref[0, e]
        vt = jnp.broadcast_to(thr_ref[pl.ds(e, 1), :], (n, n))
        nd = jnp.broadcast_to(need_ref[pl.ds(e, 1), :], (n, n))
        gt = jnp.where(a > vt, 1.0, 0.0)
        eq = jnp.where(a == vt, 1.0, 0.0).astype(BF16)
        eq_before = _dot(eq, us) + _dot(ls, _dot(eq, ones).astype(BF16))
        sel = gt + jnp.where(eq_before < nd, eq.astype(F32), 0.0)
        selb = sel.astype(BF16)
        tile_cnt = _dot(selb, ones)
        off_lo = _dot(ls, tile_cnt.astype(BF16))
        off_hi = off_lo + tile_cnt
        lo = jnp.concatenate([off_lo] * reps, axis=1)
        hi = jnp.concatenate([off_hi] * reps, axis=1)
        in_tile = jnp.where(lo <= c_full, jnp.where(c_full < hi, 1.0, 0.0), 0.0)
        incl = _dot(li, sel.T.astype(BF16))
        cnt_at = _dot(incl.astype(BF16), in_tile.astype(BF16))
        tile_of = jnp.sum(in_tile * k_full, axis=0, keepdims=True)
        rank = c_row - jnp.sum(in_tile * lo, axis=0, keepdims=True)
        local = jnp.sum(jnp.where(cnt_at <= rank, 1.0, 0.0), axis=0, keepdims=True)
        idx = jnp.where(c_row < cap, tile_of * n + local, 0.0)
        idx_ref[0, pl.ds(e, 1), :] = idx.astype(I32)
        a_rows = _dot_const_rhs(a.T, in_tile.astype(BF16), 3)
        gate = jnp.sum(jnp.where(k_full == local, a_rows, 0.0), axis=0, keepdims=True)
        gate_ref[0, pl.ds(e, 1), :] = jnp.where(c_row < cap, gate, 0.0)
        return carry

    lax.fori_loop(0, N_EXPERTS, per_expert, 0)


def _select(afft, a3, us, ls, li, cap, cps):
    b, _, p = afft.shape
    const = lambda a: pl.BlockSpec(a.shape, lambda i: (0, 0))
    return pl.pallas_call(
        functools.partial(_select_body, cap), grid=(b,),
        in_specs=[pl.BlockSpec((1, N_EXPERTS, p), lambda i: (i, 0, 0)),
                  pl.BlockSpec((1, N_EXPERTS, LANE, LANE), lambda i: (i, 0, 0, 0)),
                  const(us), const(ls), const(li)],
        out_specs=[pl.BlockSpec((1, N_EXPERTS, cps), lambda i: (i, 0, 0)),
                   pl.BlockSpec((1, N_EXPERTS, cps), lambda i: (i, 0, 0))],
        out_shape=[jax.ShapeDtypeStruct((b, N_EXPERTS, cps), I32),
                   jax.ShapeDtypeStruct((b, N_EXPERTS, cps), F32)],
        scratch_shapes=[pltpu.VMEM((N_EXPERTS, LANE), F32), pltpu.VMEM((N_EXPERTS, LANE), F32)],
        compiler_params=_cparams(("parallel",)), name="select")(afft, a3, us, ls, li)


def _ffn_body(nf, ntok, idx_ref, gate_ref, xe_ref, wg_ref, wu_ref, wd_ref, y_ref,
              xs_ref, xb_ref, acc_ref, acc3_ref, sem):
    b = pl.program_id(0)
    e = pl.program_id(1)
    f = pl.program_id(2)
    ne = pl.num_programs(1)
    rows = xb_ref.shape[0]
    cps = idx_ref.shape[3] // N_EXPERTS
    chunk = rows // nf
    slot = lax.rem(e, 2)
    tok0 = b * ntok
    tile = lambda r: pl.ds(pl.multiple_of(r * SUBLANE, SUBLANE), SUBLANE)

    def row_copy(c, token, buf):
        return pltpu.make_async_copy(xe_ref.at[tile(tok0 + token), :], xs_ref.at[buf, tile(c), :], sem.at[buf])

    def drain(buf):
        def body(c, carry):
            row_copy(c, 0, buf).wait()
            return carry
        lax.fori_loop(0, rows, body, 0, unroll=8)

    @pl.when((e == 0) & (f == 0))
    def _():
        y_ref[...] = jnp.zeros_like(y_ref)

        def issue(c, carry):
            row_copy(c, idx_ref[0, 0, 0, c], 0).start()
            return carry

        lax.fori_loop(0, rows, issue, 0, unroll=8)

    @pl.when(f == 0)
    def _():
        drain(slot)
        for a in range(D_MODEL // LANE):
            xb_ref[:, a * LANE:(a + 1) * LANE] = xs_ref[slot, pl.ds(a, rows, stride=SUBLANE), :].astype(BF16)

    nxt0 = lax.rem(e + 1, ne) * cps + f * chunk
    for u in range(chunk):
        row_copy(f * chunk + u, idx_ref[0, 0, 0, nxt0 + u], 1 - slot).start()

    x = xb_ref[...]
    a = _dot(x, wg_ref[0, 0])
    u = _dot(x, wu_ref[0, 0])
    part = _dot((_silu(a) * u).astype(BF16), wd_ref[0, 0])

    @pl.when(f == 0)
    def _():
        acc_ref[...] = part

    @pl.when(f > 0)
    def _():
        acc_ref[...] += part

    @pl.when(f == nf - 1)
    def _():
        for a in range(D_MODEL // LANE):
            acc3_ref[pl.ds(a, rows, stride=SUBLANE), :] = acc_ref[:, a * LANE:(a + 1) * LANE]

        def scatter(grp, carry):
            base = grp * SCATTER_GROUP
            off = e * cps + base
            ts = [idx_ref[0, 0, 0, off + u] for u in range(SCATTER_GROUP)]
            vals = [y_ref[0, tile(ts[u]), :] + gate_ref[0, 0, 0, off + u] * acc3_ref[tile(base + u), :]
                    for u in range(SCATTER_GROUP)]
            for u in range(SCATTER_GROUP):
                y_ref[0, tile(ts[u]), :] = vals[u]
            return carry

        lax.fori_loop(0, rows // SCATTER_GROUP, scatter, 0)

    @pl.when((e == ne - 1) & (f == nf - 1))
    def _():
        drain(1 - slot)


def _ffn(idx, gate, xe, wg, wu, wd, layer, rows):
    b, p8, _ = xe.shape
    flat = lambda a: a.reshape(b, 1, 1, -1)
    tf = 2048
    nf = EXPERT_FF // tf
    assert rows % nf == 0 and rows % SCATTER_GROUP == 0 and N_EXPERTS % 2 == 0
    smem = pl.BlockSpec((1, 1, 1, idx.shape[1] * idx.shape[2]), lambda i, e, f: (i, 0, 0, 0),
                        memory_space=pltpu.SMEM)
    return pl.pallas_call(
        functools.partial(_ffn_body, nf, p8 // SUBLANE), grid=(b, N_EXPERTS, nf),
        in_specs=[smem, smem, pl.BlockSpec(memory_space=pl.ANY),
                  pl.BlockSpec((1, 1, D_MODEL, tf), lambda i, e, f: (layer, e, 0, f)),
                  pl.BlockSpec((1, 1, D_MODEL, tf), lambda i, e, f: (layer, e, 0, f)),
                  pl.BlockSpec((1, 1, tf, D_MODEL), lambda i, e, f: (layer, e, f, 0))],
        out_specs=pl.BlockSpec((1, p8, LANE), lambda i, e, f: (i, 0, 0), pipeline_mode=pl.Buffered(1)),
        out_shape=jax.ShapeDtypeStruct((b, p8, LANE), F32),
        scratch_shapes=[pltpu.VMEM((2, rows * SUBLANE, LANE), F32), pltpu.VMEM((rows, D_MODEL), BF16),
                        pltpu.VMEM((rows, D_MODEL), F32), pltpu.VMEM((rows * SUBLANE, LANE), F32),
                        pltpu.SemaphoreType.DMA((2,))],
        compiler_params=_cparams(("arbitrary", "arbitrary", "arbitrary")), name="expert_ffn")(
            flat(idx), flat(gate), xe.reshape(b * p8, LANE), wg, wu, wd)


def _final_body(h_ref, y_ref, o_ref):
    o_ref[0] = h_ref[0] + _untile_rows(y_ref.at[0], LANE)


def _final_add(h, y, seq):
    b, p, _ = h.shape
    lead_blocks = LEAD // LANE
    src = pl.BlockSpec((1, LANE, D_MODEL), lambda i, j: (i, j + lead_blocks, 0))
    return pl.pallas_call(
        _final_body, grid=(b, seq // LANE),
        in_specs=[src, pl.BlockSpec((1, LANE * SUBLANE, LANE), lambda i, j: (i, j + lead_blocks, 0))],
        out_specs=pl.BlockSpec((1, LANE, D_MODEL), lambda i, j: (i, j, 0)),
        out_shape=jax.ShapeDtypeStruct((b, seq, D_MODEL), F32),
        compiler_params=_cparams(("parallel", "parallel")), name="final_add")(h, y)


def _rope_tables(seq):
    rows = seq // GRID_W
    row = jnp.repeat(jnp.arange(rows), GRID_W).astype(F32)
    col = (jnp.arange(rows * GRID_W) % GRID_W).astype(F32)
    n_pair = HEAD_DIM // 4
    inv = ROPE_THETA ** (-jnp.arange(n_pair, dtype=F32) / n_pair)
    ang = jnp.concatenate([row[:, None] * inv, col[:, None] * inv], axis=-1)
    ang = jnp.concatenate([jnp.zeros((LEAD, HEAD_DIM // 2), F32), ang], axis=0)
    cos, sin = jnp.cos(ang), jnp.sin(ang)
    cos64 = jnp.concatenate([cos, cos], axis=-1)
    sin64 = jnp.concatenate([-sin, sin], axis=-1)
    scale = HEAD_DIM ** -0.5
    return (jnp.tile(cos64, (1, ATT_HEADS)) * scale, jnp.tile(sin64, (1, ATT_HEADS)) * scale,
            jnp.tile(cos64, (1, ATT_KV_HEADS)), jnp.tile(sin64, (1, ATT_KV_HEADS)))


def _block_mean_matrix(width):
    i = jnp.arange(width)
    return jnp.where((i[:, None] // HEAD_DIM) == (i[None, :] // HEAD_DIM), 1.0 / HEAD_DIM, 0.0).astype(BF16)


def kernel(x, meta_tokens, norm1_w, w_in, q_norm_w, k_norm_w, attn_norm_w, hgrn_lb, hgrn_norm_w, conv_w, conv_b, dt_bias, a_log, d_skip, ssm_norm_w, w_out, norm2_w, router_w, w_gate, w_up, w_down):
    bsz, seq, _ = x.shape
    depth = norm1_w.shape[0]
    assert seq % LANE == 0 and seq % GRID_W == 0 and D_MODEL == SUBLANE * LANE
    p = LEAD + seq
    assert p // LANE <= LANE
    n_tok = N_META + seq
    cap = CAPACITY_FACTOR * n_tok // N_EXPERTS
    rows_ffn = -(-cap // 16) * 16
    cps = -(-cap // LANE) * LANE

    h = jnp.concatenate([jnp.zeros((bsz, PADF, D_MODEL), x.dtype),
                         jnp.broadcast_to(meta_tokens[None].astype(x.dtype), (bsz, N_META, D_MODEL)), x], axis=1)

    tabs = _rope_tables(seq)
    tok = jnp.arange(p)
    att_bias = jnp.where(tok[:LEAD] >= PADF, 0.0, NEG_BIG).astype(F32)[None, :]
    gq = _block_mean_matrix(ATT_WIDTH)
    gk = _block_mean_matrix(KV_WIDTH)
    ii = jnp.arange(LANE)
    tri_lo = (ii[None, :] <= ii[:, None]).astype(BF16)
    tri_up = (ii[None, :] >= ii[:, None]).astype(BF16)
    strict_up = (ii[:, None] < ii[None, :]).astype(BF16)
    strict_lo = (ii[None, :] < ii[:, None]).astype(BF16)

    xor = ii[:, None] ^ ii[None, :]
    msb = jnp.zeros_like(xor)
    for bit in range(7):
        msb = jnp.where(xor >= (1 << bit), 1 << bit, msb)
    lvl_f = jnp.where(ii[:, None] >= ii[None, :], msb, -1).astype(I32)
    lvl_b = jnp.where(ii[:, None] <= ii[None, :], msb, -1).astype(I32)

    soft = jax.nn.softmax(hgrn_lb.astype(F32), axis=1)
    lower_bounds = jnp.clip(jnp.cumsum(soft, axis=1) - soft[:, :1], 0.0, 1.0)

    wg_all, wu_all, wd_all = w_gate.astype(BF16), w_up.astype(BF16), w_down.astype(BF16)
    y = None
    for l in range(depth):
        w = w_in[l]
        o = 0
        pieces = []
        for width in (ATT_WIDTH + 2 * KV_WIDTH, 5 * HGRN_WIDTH, SSM_WIDTH, SSM_CONV_DIM, 2 * SSM_HEADS):
            pieces.append(w[:, o:o + width])
            o += width
        pieces[4] = jnp.pad(pieces[4], ((0, 0), (0, LANE - 2 * SSM_HEADS)))
        ws = [a.astype(BF16) for a in pieces]
        h2, (att_p, hg_p, z_p, xbc_p, dt_p) = _inproj(
            h.reshape(bsz * p, D_MODEL), None if y is None else y.reshape(bsz * p * SUBLANE, LANE),
            norm1_w[l][None, :], ws)
        h = h2.reshape(bsz, p, D_MODEL)
        r3 = lambda a: a.reshape(bsz, p, a.shape[-1])
        att_p, hg_p, z_p, xbc_p, dt_p = r3(att_p), r3(hg_p), r3(z_p), r3(xbc_p), r3(dt_p)

        qr, kt, vv = _attn_prep(att_p, tabs, jnp.tile(q_norm_w[l], ATT_HEADS)[None, :],
                                jnp.tile(k_norm_w[l], ATT_KV_HEADS)[None, :], gq, gk)
        att = _attention(qr, kt, vv, att_bias)

        lb_f = lower_bounds[0, l].reshape(HGRN_HEADS, 1, HGRN_DK)
        lb_b = lower_bounds[1, l].reshape(HGRN_HEADS, 1, HGRN_DK)
        o_f = _gla(hg_p, lb_f, tri_lo, lvl_f, False)
        rec = _gla(hg_p, lb_b, tri_up, lvl_b, True, o_f, hgrn_norm_w[l].reshape(HGRN_HEADS, 1, HGRN_DK))

        cw = jnp.pad(conv_w[l].T, ((0, SUBLANE - SSM_CONV), (0, 0)))
        xs, bc = _ssd_conv(xbc_p, cw, conv_b[l][None, :])
        dtb = jnp.pad(dt_bias[l].reshape(-1), (0, LANE - 2 * SSM_HEADS))[None, :]
        arow = jnp.pad(-jnp.exp(a_log[l].astype(F32)).reshape(-1), (0, LANE - 2 * SSM_HEADS))[None, :]
        y_f, y_b = _ssd(xs, bc, dt_p, dtb, arow, tri_lo, tri_up)

        wo = w_out[l].astype(BF16)
        h = _outproj(att, rec, y_f, y_b, xs, z_p, h, attn_norm_w[l][None, :], ssm_norm_w[l][None, :],
                     jnp.repeat(d_skip[l], SSM_HEAD_DIM)[None, :],
                     wo[:ATT_WIDTH], wo[ATT_WIDTH:ATT_WIDTH + HGRN_WIDTH], wo[ATT_WIDTH + HGRN_WIDTH:])

        wr = jnp.pad(router_w[l], ((0, 0), (0, LANE - N_EXPERTS)))
        wr_hi = wr.astype(BF16)
        wr_lo = (wr - wr_hi.astype(F32)).astype(BF16)
        xe, afft = _router(h, norm2_w[l][None, :], jnp.concatenate([wr_hi, wr_lo], axis=1))
        nb = p // LANE
        a3 = jnp.pad(afft.reshape(bsz, N_EXPERTS, nb, LANE), ((0, 0), (0, 0), (0, LANE - nb), (0, 0)),
                     constant_values=-1.0)
        idx, gate = _select(afft, a3, strict_up, strict_lo, tri_lo, cap, cps)
        y = _ffn(idx, gate, xe, wg_all, wu_all, wd_all, l, rows_ffn)

    return _final_add(h, y, seq)
```
